```python
import jax, jax.numpy as jnp
from jax import lax
import numpy as np

D_MODEL = 2048
BATCH = 1
SEQ = 16384
DEPTH = 1

HEAD_DIM = 128
N_Q_HEADS = D_MODEL // HEAD_DIM
N_KV_HEADS = N_Q_HEADS // 4
Q_PER_KV = N_Q_HEADS // N_KV_HEADS
ATTN_WIDTH = N_Q_HEADS * HEAD_DIM
KV_WIDTH = N_KV_HEADS * HEAD_DIM
AXIS_ROPE_DIM = HEAD_DIM // 2
ROPE_THETA = 10000.0
Q_BLOCK = 128
FOURIER_GROUPS = 8
FOURIER_GROUP_DIM = 128
FOURIER_WIDTH = FOURIER_GROUPS * FOURIER_GROUP_DIM
N_BRANCHES = 2
IN_WIDTH = ATTN_WIDTH + 2 * KV_WIDTH + FOURIER_WIDTH + N_BRANCHES * D_MODEL
SPLIT_POINTS = [ATTN_WIDTH, ATTN_WIDTH + KV_WIDTH, ATTN_WIDTH + 2 * KV_WIDTH,
                ATTN_WIDTH + 2 * KV_WIDTH + FOURIER_WIDTH]
D_FF = ((8 * D_MODEL // 3 + 255) // 256) * 256
MACARON_WEIGHT = 0.5
GRID_W = 64
EPS = 1e-6

kernel_name = "hybrid_gated_fnet_axial_gqa_macaron"


def _rms_norm(x, g):
    xf = x.astype(jnp.float32)
    y = xf * lax.rsqrt(jnp.mean(xf * xf, axis=-1, keepdims=True) + EPS)
    return (y * g.astype(jnp.float32)).astype(x.dtype)


def _swiglu(x, w_gate, w_up, w_down):
    return (jax.nn.silu(x @ w_gate) * (x @ w_up)) @ w_down


def _macaron_ffn(x, pre_g, w_gate, w_up, w_down, post_g):
    return x + MACARON_WEIGHT * _rms_norm(_swiglu(_rms_norm(x, pre_g), w_gate, w_up, w_down), post_g)


def _axial_rope_tables(seq):
    rows = seq // GRID_W
    row = jnp.repeat(jnp.arange(rows, dtype=jnp.float32), GRID_W)
    col = jnp.tile(jnp.arange(GRID_W, dtype=jnp.float32), rows)
    n_freq = AXIS_ROPE_DIM // 2
    inv_freq = ROPE_THETA ** (-jnp.arange(n_freq, dtype=jnp.float32) / n_freq)
    ang_r = row[:, None] * inv_freq
    ang_c = col[:, None] * inv_freq
    return (jnp.cos(ang_r), jnp.sin(ang_r), jnp.cos(ang_c), jnp.sin(ang_c))


def _rotate(xs, cos, sin):
    x1, x2 = jnp.split(xs.astype(jnp.float32), 2, axis=-1)
    c = cos[None, :, None, :]
    s = sin[None, :, None, :]
    return jnp.concatenate([x1 * c - x2 * s, x2 * c + x1 * s], axis=-1)


def _apply_axial_rope(x, tables):
    cos_r, sin_r, cos_c, sin_c = tables
    x_row, x_col = jnp.split(x, 2, axis=-1)
    return jnp.concatenate([_rotate(x_row, cos_r, sin_r), _rotate(x_col, cos_c, sin_c)], axis=-1).astype(x.dtype)


def _blocked_gqa(q, k, v):
    b, s = q.shape[0], q.shape[1]
    n_blk = s // Q_BLOCK
    qb = q.reshape(b, n_blk, Q_BLOCK, N_KV_HEADS, Q_PER_KV, HEAD_DIM).transpose(1, 0, 2, 3, 4, 5)
    scale = HEAD_DIM ** -0.5

    def attend(q_blk):
        sc = jnp.einsum('bqkgd,bskd->bkgqs', q_blk, k, preferred_element_type=jnp.float32) * scale
        p = jax.nn.softmax(sc, axis=-1)
        return jnp.einsum('bkgqs,bskd->bqkgd', p.astype(v.dtype), v)

    o = lax.map(attend, qb)
    return o.transpose(1, 0, 2, 3, 4, 5).reshape(b, s, ATTN_WIDTH)


def _fourier_mix(f):
    b, s = f.shape[0], f.shape[1]
    fg = f.astype(jnp.float32).reshape(b, s, FOURIER_GROUPS, FOURIER_GROUP_DIM)
    mixed = jnp.real(jnp.fft.fft2(fg, axes=(1, 3), norm='ortho'))
    return mixed.reshape(b, s, FOURIER_WIDTH).astype(f.dtype)


def _gated_mixers(h, pre_g, w_in, b_gate, q_norm_g, k_norm_g, w_attn_o, w_fourier, w_out, post_g, rope):
    b, s, _ = h.shape
    u = _rms_norm(h, pre_g)
    q, k, v, f, g = jnp.split(u @ w_in, SPLIT_POINTS, axis=-1)
    q = _apply_axial_rope(_rms_norm(q.reshape(b, s, N_Q_HEADS, HEAD_DIM), q_norm_g), rope)
    k = _apply_axial_rope(_rms_norm(k.reshape(b, s, N_KV_HEADS, HEAD_DIM), k_norm_g), rope)
    v = v.reshape(b, s, N_KV_HEADS, HEAD_DIM)
    y_attn = _blocked_gqa(q, k, v) @ w_attn_o
    y_four = _fourier_mix(f) @ w_fourier
    gates = jax.nn.sigmoid((g.reshape(b, s, N_BRANCHES, D_MODEL) + b_gate).astype(jnp.float32)).astype(h.dtype)
    merged = gates[:, :, 0] * y_attn + gates[:, :, 1] * y_four
    return h + _rms_norm(merged @ w_out, post_g)


def setup_inputs(seed: int = 0) -> dict:
    key = jax.random.key(seed)
    ks = jax.random.split(key, 21)

    def w(k, shape, fan_in):
        return jax.random.normal(k, shape, jnp.float32) * (fan_in ** -0.5)

    def gain(k, shape):
        return 1.0 + 0.05 * jax.random.normal(k, shape, jnp.float32)

    L = DEPTH
    return {
        "x": jax.random.normal(ks[0], (BATCH, SEQ, D_MODEL), jnp.float32),
        "ffn1_pre_g": gain(ks[1], (L, D_MODEL)),
        "ffn1_w_gate": w(ks[2], (L, D_MODEL, D_FF), D_MODEL),
        "ffn1_w_up": w(ks[3], (L, D_MODEL, D_FF), D_MODEL),
        "ffn1_w_down": w(ks[4], (L, D_FF, D_MODEL), D_FF),
        "ffn1_post_g": gain(ks[5], (L, D_MODEL)),
        "mix_pre_g": gain(ks[6], (L, D_MODEL)),
        "w_in": w(ks[7], (L, D_MODEL, IN_WIDTH), D_MODEL),
        "b_gate": 0.1 * jax.random.normal(ks[8], (L, N_BRANCHES, D_MODEL), jnp.float32),
        "q_norm_g": gain(ks[9], (L, HEAD_DIM)),
        "k_norm_g": gain(ks[10], (L, HEAD_DIM)),
        "w_attn_o": w(ks[11], (L, ATTN_WIDTH, D_MODEL), ATTN_WIDTH),
        "w_fourier": w(ks[12], (L, FOURIER_WIDTH, D_MODEL), FOURIER_WIDTH),
        "w_out": w(ks[13], (L, D_MODEL, D_MODEL), D_MODEL),
        "mix_post_g": gain(ks[14], (L, D_MODEL)),
        "ffn2_pre_g": gain(ks[15], (L, D_MODEL)),
        "ffn2_w_gate": w(ks[16], (L, D_MODEL, D_FF), D_MODEL),
        "ffn2_w_up": w(ks[17], (L, D_MODEL, D_FF), D_MODEL),
        "ffn2_w_down": w(ks[18], (L, D_FF, D_MODEL), D_FF),
        "ffn2_post_g": gain(ks[19], (L, D_MODEL)),
    }


def reference(x, ffn1_pre_g, ffn1_w_gate, ffn1_w_up, ffn1_w_down, ffn1_post_g,
              mix_pre_g, w_in, b_gate, q_norm_g, k_norm_g, w_attn_o, w_fourier, w_out, mix_post_g,
              ffn2_pre_g, ffn2_w_gate, ffn2_w_up, ffn2_w_down, ffn2_post_g):
    rope = _axial_rope_tables(x.shape[1])
    h = x
    for l in range(DEPTH):
        h = _macaron_ffn(h, ffn1_pre_g[l], ffn1_w_gate[l], ffn1_w_up[l], ffn1_w_down[l], ffn1_post_g[l])
        h = _gated_mixers(h, mix_pre_g[l], w_in[l], b_gate[l], q_norm_g[l], k_norm_g[l],
                          w_attn_o[l], w_fourier[l], w_out[l], mix_post_g[l], rope)
        h = _macaron_ffn(h, ffn2_pre_g[l], ffn2_w_gate[l], ffn2_w_up[l], ffn2_w_down[l], ffn2_post_g[l])
    return h
```

```python
import functools
import math

import jax
import jax.numpy as jnp
from jax import lax
from jax.experimental import pallas as pl
from jax.experimental.pallas import tpu as pltpu

HEAD_DIM = 128
N_KV_HEADS = 4
Q_PER_KV = 4
N_Q_HEADS = N_KV_HEADS * Q_PER_KV
ATTN_WIDTH = N_Q_HEADS * HEAD_DIM
KV_WIDTH = N_KV_HEADS * HEAD_DIM
FOURIER_GROUPS = 8
FOURIER_GROUP_DIM = 128
FOURIER_WIDTH = FOURIER_GROUPS * FOURIER_GROUP_DIM
ROPE_HALF = HEAD_DIM // 4
ROPE_THETA = 10000.0
GRID_W = 64
EPS = 1e-6
MACARON_WEIGHT = 0.5
LOG2_E = 1.4426950408889634

V7X_VMEM_BYTES = 64 * 1024 * 1024
V7X_LANES = 128
SEQ_DFT_MINOR = 128

BF16 = jnp.bfloat16
F32 = jnp.float32


def _vmem_limit(estimate_bytes):
    return int(min(estimate_bytes * 5 // 4 + (4 << 20), V7X_VMEM_BYTES - (6 << 20)))


def _params(estimate_bytes, n_axes):
    return pltpu.CompilerParams(
        dimension_semantics=("arbitrary",) * n_axes,
        vmem_limit_bytes=_vmem_limit(estimate_bytes),
    )


def _rms(x, g):
    ms = jnp.mean(x * x, axis=-1, keepdims=True)
    return x * lax.rsqrt(ms + EPS) * g


def _ffn_body(x_ref, pre_g_ref, wg_ref, wu_ref, wd_ref, post_g_ref, o_ref, xn_ref):
    j = pl.program_id(1)

    @pl.when(j == 0)
    def _():
        xn_ref[...] = _rms(x_ref[...], pre_g_ref[...]).astype(xn_ref.dtype)

    xn = xn_ref[...]
    gate = jnp.dot(xn, wg_ref[...], preferred_element_type=F32)
    up = jnp.dot(xn, wu_ref[...], preferred_element_type=F32)
    hidden = (gate * jax.nn.sigmoid(gate) * up).astype(BF16)
    part = jnp.dot(hidden, wd_ref[...], preferred_element_type=F32)

    @pl.when(j == 0)
    def _():
        o_ref[...] = part

    @pl.when(j > 0)
    def _():
        o_ref[...] += part

    @pl.when(j == pl.num_programs(1) - 1)
    def _():
        o_ref[...] = x_ref[...] + MACARON_WEIGHT * _rms(o_ref[...], post_g_ref[...])


def _ffn(x, pre_g, w_gate, w_up, w_down, post_g, *, tm, tf):
    s, d = x.shape
    d_ff = w_gate.shape[1]
    est = 2 * tm * d * 4 * 2 + tm * d * 2 + 2 * 3 * d * tf * 2 + 3 * tm * tf * 4 + tm * d * 4
    return pl.pallas_call(
        _ffn_body,
        grid=(s // tm, d_ff // tf),
        in_specs=[
            pl.BlockSpec((tm, d), lambda i, j: (i, 0)),
            pl.BlockSpec((1, d), lambda i, j: (0, 0)),
            pl.BlockSpec((d, tf), lambda i, j: (0, j)),
            pl.BlockSpec((d, tf), lambda i, j: (0, j)),
            pl.BlockSpec((tf, d), lambda i, j: (j, 0)),
            pl.BlockSpec((1, d), lambda i, j: (0, 0)),
        ],
        out_specs=pl.BlockSpec((tm, d), lambda i, j: (i, 0)),
        out_shape=jax.ShapeDtypeStruct((s, d), F32),
        scratch_shapes=[pltpu.VMEM((tm, d), BF16)],
        compiler_params=_params(est, 2),
        name="macaron_ffn",
    )(x, pre_g, w_gate, w_up, w_down, post_g)


def _norm_rope(yh, g, cos, sin, scale):
    yn = _rms(yh, g)
    lane = lax.broadcasted_iota(jnp.int32, yn.shape, 1)
    first_half = (lane % (2 * ROPE_HALF)) < ROPE_HALF
    partner = jnp.where(first_half,
                        pltpu.roll(yn, HEAD_DIM - ROPE_HALF, 1),
                        pltpu.roll(yn, ROPE_HALF, 1))
    out = yn * cos + partner * sin
    return out if scale is None else out * scale


def _inproj_body(x_ref, pre_g_ref, w_ref, bias_ref, qg_ref, kg_ref, cos_ref, sin_ref, o_ref, xn_ref,
                 *, tn, q_scale):
    j = pl.program_id(1)
    n_q_tiles = ATTN_WIDTH // tn
    kv_tile = n_q_tiles
    f_tile = kv_tile + 1
    heads_per_tile = tn // HEAD_DIM

    @pl.when(j == 0)
    def _():
        xn_ref[...] = _rms(x_ref[...], pre_g_ref[...]).astype(xn_ref.dtype)

    y = jnp.dot(xn_ref[...], w_ref[...], preferred_element_type=F32)

    def head(h):
        return y[:, h * HEAD_DIM:(h + 1) * HEAD_DIM]

    @pl.when(j < n_q_tiles)
    def _():
        for h in range(heads_per_tile):
            o_ref[:, h * HEAD_DIM:(h + 1) * HEAD_DIM] = _norm_rope(
                head(h), qg_ref[...], cos_ref[...], sin_ref[...], q_scale).astype(o_ref.dtype)

    @pl.when(j == kv_tile)
    def _():
        for h in range(N_KV_HEADS):
            o_ref[:, h * HEAD_DIM:(h + 1) * HEAD_DIM] = _norm_rope(
                head(h), kg_ref[...], cos_ref[...], sin_ref[...], None).astype(o_ref.dtype)
        o_ref[:, KV_WIDTH:] = y[:, KV_WIDTH:].astype(o_ref.dtype)

    @pl.when(j == f_tile)
    def _():
        o_ref[...] = y.astype(o_ref.dtype)

    @pl.when(j > f_tile)
    def _():
        o_ref[...] = jax.nn.sigmoid(y + bias_ref[...]).astype(o_ref.dtype)


def _in_proj(h, pre_g, w_in, bias_full, q_g, k_g, cos_tab, sin_tab, *, tm, tn):
    s, d = h.shape
    n = w_in.shape[1]
    assert tn == 2 * KV_WIDTH == FOURIER_WIDTH and ATTN_WIDTH % tn == 0
    q_scale = (HEAD_DIM ** -0.5) * LOG2_E
    est = 2 * tm * d * 4 + tm * d * 2 + 2 * d * tn * 2 + 2 * tm * tn * 2 + 2 * tm * tn * 4 + 4 * tm * HEAD_DIM * 4
    return pl.pallas_call(
        functools.partial(_inproj_body, tn=tn, q_scale=q_scale),
        grid=(s // tm, n // tn),
        in_specs=[
            pl.BlockSpec((tm, d), lambda i, j: (i, 0)),
            pl.BlockSpec((1, d), lambda i, j: (0, 0)),
            pl.BlockSpec((d, tn), lambda i, j: (0, j)),
            pl.BlockSpec((1, tn), lambda i, j: (0, j)),
            pl.BlockSpec((1, HEAD_DIM), lambda i, j: (0, 0)),
            pl.BlockSpec((1, HEAD_DIM), lambda i, j: (0, 0)),
            pl.BlockSpec((tm, HEAD_DIM), lambda i, j: (i, 0)),
            pl.BlockSpec((tm, HEAD_DIM), lambda i, j: (i, 0)),
        ],
        out_specs=pl.BlockSpec((tm, tn), lambda i, j: (i, j)),
        out_shape=jax.ShapeDtypeStruct((s, n), BF16),
        scratch_shapes=[pltpu.VMEM((tm, d), BF16)],
        compiler_params=_params(est, 2),
        name="in_proj",
    )(h, pre_g, w_in, bias_full, q_g, k_g, cos_tab, sin_tab)


def _rope_tables(seq):
    t = jnp.arange(seq, dtype=jnp.int32)
    row = (t // GRID_W).astype(F32)
    col = (t % GRID_W).astype(F32)
    n_freq = ROPE_HALF
    inv_freq = ROPE_THETA ** (-jnp.arange(n_freq, dtype=F32) / n_freq)
    ang_r = row[:, None] * inv_freq
    ang_c = col[:, None] * inv_freq
    cos_tab = jnp.concatenate([jnp.cos(ang_r), jnp.cos(ang_r), jnp.cos(ang_c), jnp.cos(ang_c)], axis=-1)
    sin_tab = jnp.concatenate([-jnp.sin(ang_r), jnp.sin(ang_r), -jnp.sin(ang_c), jnp.sin(ang_c)], axis=-1)
    return cos_tab, sin_tab


def _attn_body(q_ref, k_ref, vt_ref, o_ref, m_ref, l_ref, acc_ref, *, tk):
    n_chunks = vt_ref.shape[0]
    m_ref[...] = jnp.full(m_ref.shape, -jnp.inf, F32)
    l_ref[...] = jnp.zeros(l_ref.shape, F32)
    acc_ref[...] = jnp.zeros(acc_ref.shape, F32)

    def chunk(c, carry):
        start = pl.multiple_of(c * tk, tk)
        k_c = k_ref[pl.ds(start, tk), :]
        vt_c = vt_ref[c]
        for h in range(Q_PER_KV):
            q_h = q_ref[:, h * HEAD_DIM:(h + 1) * HEAD_DIM]
            s_t = lax.dot_general(k_c, q_h, (((1,), (1,)), ((), ())), preferred_element_type=F32)
            m_old = m_ref[h]
            m_new = jnp.maximum(m_old, jnp.max(s_t, axis=0, keepdims=True))
            alpha = jnp.exp2(m_old - m_new)
            p_t = jnp.exp2(s_t - m_new)
            l_ref[h] = alpha * l_ref[h] + jnp.sum(p_t, axis=0, keepdims=True)
            acc_ref[h] = alpha * acc_ref[h] + jnp.dot(vt_c, p_t.astype(BF16), preferred_element_type=F32)
            m_ref[h] = m_new
        return carry

    lax.fori_loop(0, n_chunks, chunk, 0)

    for h in range(Q_PER_KV):
        o_t = acc_ref[h] / l_ref[h]
        o_ref[:, h * HEAD_DIM:(h + 1) * HEAD_DIM] = o_t.T.astype(o_ref.dtype)


def _attention(proj, vt, *, tq, tk):
    s = proj.shape[0]
    group_w = Q_PER_KV * HEAD_DIM
    k_block0 = ATTN_WIDTH // HEAD_DIM
    est = (2 * tq * group_w * 2 * 2 + 2 * s * HEAD_DIM * 2 * 2 + Q_PER_KV * (HEAD_DIM + 16) * tq * 4
           + 4 * tk * tq * 4)
    return pl.pallas_call(
        functools.partial(_attn_body, tk=tk),
        grid=(N_KV_HEADS, s // tq),
        in_specs=[
            pl.BlockSpec((tq, group_w), lambda g, i: (i, g)),
            pl.BlockSpec((s, HEAD_DIM), lambda g, i: (0, k_block0 + g)),
            pl.BlockSpec((None, s // tk, HEAD_DIM, tk), lambda g, i: (g, 0, 0, 0)),
        ],
        out_specs=pl.BlockSpec((tq, group_w), lambda g, i: (i, g)),
        out_shape=jax.ShapeDtypeStruct((s, ATTN_WIDTH), BF16),
        scratch_shapes=[
            pltpu.VMEM((Q_PER_KV, 1, tq), F32),
            pltpu.VMEM((Q_PER_KV, 1, tq), F32),
            pltpu.VMEM((Q_PER_KV, HEAD_DIM, tq), F32),
        ],
        compiler_params=_params(est, 2),
        name="gqa_attention",
    )(proj, proj, vt)


def _fourier1_body(f_ref, chan_ref, t_ref, y_ref, ab_ref, *, tb):
    n1 = f_ref.shape[0]
    for b in range(tb):
        for g in range(FOURIER_GROUPS):
            lo = b * FOURIER_WIDTH + g * FOURIER_GROUP_DIM
            ab = jnp.dot(f_ref[:, lo:lo + FOURIER_GROUP_DIM], chan_ref[...], preferred_element_type=F32)
            cols = slice(g * FOURIER_GROUP_DIM, (g + 1) * FOURIER_GROUP_DIM)
            ab_ref[:n1, cols] = ab[:, :FOURIER_GROUP_DIM].astype(ab_ref.dtype)
            ab_ref[n1:, cols] = ab[:, FOURIER_GROUP_DIM:].astype(ab_ref.dtype)
        y = jnp.dot(t_ref[b], ab_ref[...], preferred_element_type=F32)
        y_ref[0, b] = y[:n1].astype(y_ref.dtype)
        y_ref[1, b] = y[n1:].astype(y_ref.dtype)


def _fourier2_body(y_ref, f2_ref, o_ref, *, scale):
    two, n2, w = y_ref.shape
    y = y_ref[...].reshape(two * n2, w)
    o_ref[...] = (jnp.dot(f2_ref[...], y, preferred_element_type=F32) * scale).astype(o_ref.dtype)


def _dft_tables(seq):
    n2 = SEQ_DFT_MINOR
    n1 = seq // n2
    idx = jnp.arange(n2, dtype=jnp.int32)
    ang = (2.0 * math.pi / n2) * ((idx[:, None] * idx[None, :]) % n2).astype(F32)
    cm, sm = jnp.cos(ang), jnp.sin(ang)
    chan = jnp.concatenate([cm, sm], axis=1).astype(BF16)
    minor = jnp.concatenate([cm, sm], axis=1).astype(BF16)
    k1 = jnp.arange(n1, dtype=jnp.int32)[None, :, None]
    nn1 = jnp.arange(n1, dtype=jnp.int32)[None, None, :]
    nn2 = jnp.arange(n2, dtype=jnp.int32)[:, None, None]
    r = (k1 * (n2 * nn1 + nn2)) % seq
    a = (2.0 * math.pi / seq) * r.astype(F32)
    tc, ts = jnp.cos(a), jnp.sin(a)
    top = jnp.concatenate([tc, -ts], axis=2)
    bot = jnp.concatenate([-ts, -tc], axis=2)
    stage1 = jnp.concatenate([top, bot], axis=1).astype(BF16)
    return chan, stage1, minor


def _fourier(f, chan, stage1, minor, *, tb):
    s, w = f.shape
    n2 = SEQ_DFT_MINOR
    n1 = s // n2
    f_view = f.reshape(n1, n2 * w)
    est1 = 2 * n1 * tb * w * 2 + 2 * tb * 4 * n1 * n1 * 2 + 2 * 2 * tb * n1 * w * 2 + 2 * n1 * w * 2 + 4 * n1 * w * 4
    y = pl.pallas_call(
        functools.partial(_fourier1_body, tb=tb),
        grid=(n2 // tb,),
        in_specs=[
            pl.BlockSpec((n1, tb * w), lambda j: (0, j)),
            pl.BlockSpec((FOURIER_GROUP_DIM, 2 * FOURIER_GROUP_DIM), lambda j: (0, 0)),
            pl.BlockSpec((tb, 2 * n1, 2 * n1), lambda j: (j, 0, 0)),
        ],
        out_specs=pl.BlockSpec((2, tb, n1, w), lambda j: (0, j, 0, 0)),
        out_shape=jax.ShapeDtypeStruct((2, n2, n1, w), BF16),
        scratch_shapes=[pltpu.VMEM((2 * n1, w), BF16)],
        compiler_params=_params(est1, 1),
        name="fourier_stage1",
    )(f_view, chan, stage1)

    tk1 = tb
    y_view = y.reshape(2, n2, n1 * w)
    scale = 1.0 / math.sqrt(float(s) * FOURIER_GROUP_DIM)
    est2 = 2 * 2 * n2 * tk1 * w * 2 + 2 * n2 * tk1 * w * 2 + 2 * n2 * tk1 * w * 4
    out = pl.pallas_call(
        functools.partial(_fourier2_body, scale=scale),
        grid=(n1 // tk1,),
        in_specs=[
            pl.BlockSpec((2, n2, tk1 * w), lambda j: (0, 0, j)),
            pl.BlockSpec((n2, 2 * n2), lambda j: (0, 0)),
        ],
        out_specs=pl.BlockSpec((n2, tk1 * w), lambda j: (0, j)),
        out_shape=jax.ShapeDtypeStruct((n2, n1 * w), BF16),
        compiler_params=_params(est2, 1),
        name="fourier_stage2",
    )(y_view, minor)
    return out.reshape(s, w)


def _merge_body(a_ref, f_ref, ga_ref, gf_ref, h_ref, wa_ref, wf_ref, wo_ref, post_g_ref, o_ref):
    ya = jnp.dot(a_ref[...], wa_ref[...], preferred_element_type=F32)
    yf = jnp.dot(f_ref[...], wf_ref[...], preferred_element_type=F32)
    merged = ga_ref[...].astype(F32) * ya + gf_ref[...].astype(F32) * yf
    z = jnp.dot(merged.astype(BF16), wo_ref[...], preferred_element_type=F32)
    o_ref[...] = h_ref[...] + _rms(z, post_g_ref[...])


def _merge(attn, mixed, proj, h, w_attn_o, w_fourier, w_out, post_g, *, tm):
    s, d = h.shape
    gate_block0 = (ATTN_WIDTH + 2 * KV_WIDTH + FOURIER_WIDTH) // d
    resident = pl.Buffered(1)
    est = ((ATTN_WIDTH + FOURIER_WIDTH + d) * d * 2
           + 2 * tm * (ATTN_WIDTH + FOURIER_WIDTH + 2 * d) * 2 + 2 * 2 * tm * d * 4 + 4 * tm * d * 4)
    return pl.pallas_call(
        _merge_body,
        grid=(s // tm,),
        in_specs=[
            pl.BlockSpec((tm, ATTN_WIDTH), lambda i: (i, 0)),
            pl.BlockSpec((tm, FOURIER_WIDTH), lambda i: (i, 0)),
            pl.BlockSpec((tm, d), lambda i: (i, gate_block0)),
            pl.BlockSpec((tm, d), lambda i: (i, gate_block0 + 1)),
            pl.BlockSpec((tm, d), lambda i: (i, 0)),
            pl.BlockSpec((ATTN_WIDTH, d), lambda i: (0, 0), pipeline_mode=resident),
            pl.BlockSpec((FOURIER_WIDTH, d), lambda i: (0, 0), pipeline_mode=resident),
            pl.BlockSpec((d, d), lambda i: (0, 0), pipeline_mode=resident),
            pl.BlockSpec((1, d), lambda i: (0, 0)),
        ],
        out_specs=pl.BlockSpec((tm, d), lambda i: (i, 0)),
        out_shape=jax.ShapeDtypeStruct((s, d), F32),
        compiler_params=_params(est, 1),
        name="gated_merge",
    )(attn, mixed, proj, proj, h, w_attn_o, w_fourier, w_out, post_g)


def _tiles(seq):
    return dict(
        ffn_tm=min(512, seq), ffn_tf=512,
        proj_tm=min(1024, seq), proj_tn=1024,
        attn_tq=min(256, seq), attn_tk=min(512, seq),
        four_tb=8,
        merge_tm=min(256, seq),
    )


def kernel(x, ffn1_pre_g, ffn1_w_gate, ffn1_w_up, ffn1_w_down, ffn1_post_g, mix_pre_g, w_in, b_gate, q_norm_g, k_norm_g, w_attn_o, w_fourier, w_out, mix_post_g, ffn2_pre_g, ffn2_w_gate, ffn2_w_up, ffn2_w_down, ffn2_post_g):
    batch, seq, d = x.shape
    assert batch == 1 and seq % SEQ_DFT_MINOR == 0
    depth = w_in.shape[0]
    t = _tiles(seq)
    cos_tab, sin_tab = _rope_tables(seq)
    chan, stage1, minor = _dft_tables(seq)
    row = lambda v: v.reshape(1, -1).astype(F32)

    h = x.reshape(seq, d)
    for l in range(depth):
        h = _ffn(h, row(ffn1_pre_g[l]), ffn1_w_gate[l].astype(BF16), ffn1_w_up[l].astype(BF16),
                 ffn1_w_down[l].astype(BF16), row(ffn1_post_g[l]), tm=t["ffn_tm"], tf=t["ffn_tf"])

        n_in = w_in.shape[2]
        bias_full = jnp.concatenate([jnp.zeros((1, n_in - 2 * d), F32), row(b_gate[l])], axis=1)
        proj = _in_proj(h, row(mix_pre_g[l]), w_in[l].astype(BF16), bias_full, row(q_norm_g[l]), row(k_norm_g[l]),
                        cos_tab, sin_tab, tm=t["proj_tm"], tn=t["proj_tn"])

        tk = t["attn_tk"]
        v = proj[:, ATTN_WIDTH + KV_WIDTH:ATTN_WIDTH + 2 * KV_WIDTH]
        vt = v.reshape(seq // tk, tk, N_KV_HEADS, HEAD_DIM).transpose(2, 0, 3, 1)
        attn = _attention(proj, vt, tq=t["attn_tq"], tk=tk)

        f_col0 = ATTN_WIDTH + 2 * KV_WIDTH
        mixed = _fourier(proj[:, f_col0:f_col0 + FOURIER_WIDTH], chan, stage1, minor, tb=t["four_tb"])

        h = _merge(attn, mixed, proj, h, w_attn_o[l].astype(BF16), w_fourier[l].astype(BF16),
                   w_out[l].astype(BF16), row(mix_post_g[l]), tm=t["merge_tm"])

        h = _ffn(h, row(ffn2_pre_g[l]), ffn2_w_gate[l].astype(BF16), ffn2_w_up[l].astype(BF16),
                 ffn2_w_down[l].astype(BF16), row(ffn2_post_g[l]), tm=t["ffn_tm"], tf=t["ffn_tf"])
    return h.reshape(batch, seq, d)
```

```python
import functools
import math

import jax
import jax.numpy as jnp
from jax import lax
from jax.experimental import pallas as pl
from jax.experimental.pallas import tpu as pltpu

HEAD_DIM = 128
N_KV_HEADS = 4
Q_PER_KV = 4
N_Q_HEADS = N_KV_HEADS * Q_PER_KV
ATTN_WIDTH = N_Q_HEADS * HEAD_DIM
KV_WIDTH = N_KV_HEADS * HEAD_DIM
FOURIER_GROUPS = 8
FOURIER_GROUP_DIM = 128
FOURIER_WIDTH = FOURIER_GROUPS * FOURIER_GROUP_DIM
ROPE_HALF = HEAD_DIM // 4
ROPE_THETA = 10000.0
GRID_W = 64
EPS = 1e-6
MACARON_WEIGHT = 0.5
LOG2_E = 1.4426950408889634

V7X_VMEM_BYTES = 64 * 1024 * 1024
V7X_LANES = 128
SEQ_DFT_MINOR = 128

BF16 = jnp.bfloat16
F32 = jnp.float32


def _vmem_limit(estimate_bytes):
    return int(min(estimate_bytes * 5 // 4 + (4 << 20), V7X_VMEM_BYTES - (6 << 20)))


def _params(estimate_bytes, n_axes):
    return pltpu.CompilerParams(
        dimension_semantics=("arbitrary",) * n_axes,
        vmem_limit_bytes=_vmem_limit(estimate_bytes),
    )


def _rms(x, g):
    ms = jnp.mean(x * x, axis=-1, keepdims=True)
    return x * lax.rsqrt(ms + EPS) * g


def _ffn_body(x_ref, pre_g_ref, wg_ref, wu_ref, wd_ref, post_g_ref, o_ref, xn_ref):
    j = pl.program_id(1)

    @pl.when(j == 0)
    def _():
        xn_ref[...] = _rms(x_ref[...], pre_g_ref[...]).astype(xn_ref.dtype)
        o_ref[...] = jnp.zeros(o_ref.shape, o_ref.dtype)

    xn = xn_ref[...]
    gate = jnp.dot(xn, wg_ref[...], preferred_element_type=F32)
    up = jnp.dot(xn, wu_ref[...], preferred_element_type=F32)
    hidden = (gate * jax.nn.sigmoid(gate) * up).astype(BF16)
    o_ref[...] += jnp.dot(hidden, wd_ref[...], preferred_element_type=F32)

    @pl.when(j == pl.num_programs(1) - 1)
    def _():
        o_ref[...] = x_ref[...] + MACARON_WEIGHT * _rms(o_ref[...], post_g_ref[...])


def _ffn(x, pre_g, w_gate, w_up, w_down, post_g, *, tm, tf):
    s, d = x.shape
    d_ff = w_gate.shape[1]
    est = 2 * tm * d * 4 * 2 + tm * d * 2 + 2 * 3 * d * tf * 2 + 3 * tm * tf * 4 + tm * d * 4
    return pl.pallas_call(
        _ffn_body,
        grid=(s // tm, d_ff // tf),
        in_specs=[
            pl.BlockSpec((tm, d), lambda i, j: (i, 0)),
            pl.BlockSpec((1, d), lambda i, j: (0, 0)),
            pl.BlockSpec((d, tf), lambda i, j: (0, j)),
            pl.BlockSpec((d, tf), lambda i, j: (0, j)),
            pl.BlockSpec((tf, d), lambda i, j: (j, 0)),
            pl.BlockSpec((1, d), lambda i, j: (0, 0)),
        ],
        out_specs=pl.BlockSpec((tm, d), lambda i, j: (i, 0)),
        out_shape=jax.ShapeDtypeStruct((s, d), F32),
        scratch_shapes=[pltpu.VMEM((tm, d), BF16)],
        compiler_params=_params(est, 2),
        name="macaron_ffn",
    )(x, pre_g, w_gate, w_up, w_down, post_g)


def _norm_rope(yh, g, cos, sin, scale):
    yn = _rms(yh, g)
    lane = lax.broadcasted_iota(jnp.int32, yn.shape, 1)
    first_half = (lane % (2 * ROPE_HALF)) < ROPE_HALF
    partner = jnp.where(first_half,
                        pltpu.roll(yn, HEAD_DIM - ROPE_HALF, 1),
                        pltpu.roll(yn, ROPE_HALF, 1))
    out = yn * cos + partner * sin
    return out if scale is None else out * scale


def _inproj_body(x_ref, pre_g_ref, w_ref, bias_ref, qg_ref, kg_ref, cos_ref, sin_ref, o_ref, xn_ref,
                 *, tn, q_scale):
    j = pl.program_id(1)
    n_q_tiles = ATTN_WIDTH // tn
    kv_tile = n_q_tiles
    f_tile = kv_tile + 1
    heads_per_tile = tn // HEAD_DIM

    @pl.when(j == 0)
    def _():
        xn_ref[...] = _rms(x_ref[...], pre_g_ref[...]).astype(xn_ref.dtype)

    y = jnp.dot(xn_ref[...], w_ref[...], preferred_element_type=F32)

    def head(h):
        return y[:, h * HEAD_DIM:(h + 1) * HEAD_DIM]

    @pl.when(j < n_q_tiles)
    def _():
        for h in range(heads_per_tile):
            o_ref[:, h * HEAD_DIM:(h + 1) * HEAD_DIM] = _norm_rope(
                head(h), qg_ref[...], cos_ref[...], sin_ref[...], q_scale).astype(o_ref.dtype)

    @pl.when(j == kv_tile)
    def _():
        for h in range(N_KV_HEADS):
            o_ref[:, h * HEAD_DIM:(h + 1) * HEAD_DIM] = _norm_rope(
                head(h), kg_ref[...], cos_ref[...], sin_ref[...], None).astype(o_ref.dtype)
        o_ref[:, KV_WIDTH:] = y[:, KV_WIDTH:].astype(o_ref.dtype)

    @pl.when(j == f_tile)
    def _():
        o_ref[...] = y.astype(o_ref.dtype)

    @pl.when(j > f_tile)
    def _():
        o_ref[...] = jax.nn.sigmoid(y + bias_ref[...]).astype(o_ref.dtype)


def _in_proj(h, pre_g, w_in, bias_full, q_g, k_g, cos_tab, sin_tab, *, tm, tn):
    s, d = h.shape
    n = w_in.shape[1]
    assert tn == 2 * KV_WIDTH == FOURIER_WIDTH and ATTN_WIDTH % tn == 0
    q_scale = (HEAD_DIM ** -0.5) * LOG2_E
    est = 2 * tm * d * 4 + tm * d * 2 + 2 * d * tn * 2 + 2 * tm * tn * 2 + 2 * tm * tn * 4 + 4 * tm * HEAD_DIM * 4
    return pl.pallas_call(
        functools.partial(_inproj_body, tn=tn, q_scale=q_scale),
        grid=(s // tm, n // tn),
        in_specs=[
            pl.BlockSpec((tm, d), lambda i, j: (i, 0)),
            pl.BlockSpec((1, d), lambda i, j: (0, 0)),
            pl.BlockSpec((d, tn), lambda i, j: (0, j)),
            pl.BlockSpec((1, tn), lambda i, j: (0, j)),
            pl.BlockSpec((1, HEAD_DIM), lambda i, j: (0, 0)),
            pl.BlockSpec((1, HEAD_DIM), lambda i, j: (0, 0)),
            pl.BlockSpec((tm, HEAD_DIM), lambda i, j: (i, 0)),
            pl.BlockSpec((tm, HEAD_DIM), lambda i, j: (i, 0)),
        ],
        out_specs=pl.BlockSpec((tm, tn), lambda i, j: (i, j)),
        out_shape=jax.ShapeDtypeStruct((s, n), BF16),
        scratch_shapes=[pltpu.VMEM((tm, d), BF16)],
        compiler_params=_params(est, 2),
        name="in_proj",
    )(h, pre_g, w_in, bias_full, q_g, k_g, cos_tab, sin_tab)


def _rope_tables(seq):
    t = jnp.arange(seq, dtype=jnp.int32)
    row = (t // GRID_W).astype(F32)
    col = (t % GRID_W).astype(F32)
    n_freq = ROPE_HALF
    inv_freq = ROPE_THETA ** (-jnp.arange(n_freq, dtype=F32) / n_freq)
    ang_r = row[:, None] * inv_freq
    ang_c = col[:, None] * inv_freq
    cos_tab = jnp.concatenate([jnp.cos(ang_r), jnp.cos(ang_r), jnp.cos(ang_c), jnp.cos(ang_c)], axis=-1)
    sin_tab = jnp.concatenate([-jnp.sin(ang_r), jnp.sin(ang_r), -jnp.sin(ang_c), jnp.sin(ang_c)], axis=-1)
    return cos_tab, sin_tab


HEAD_AUG = HEAD_DIM + 16
SCORE_BOUND_LIMIT = 100.0


def _attn_body_bounded(q_ref, k_ref, vt_ref, o_ref, acc_ref, *, tk):
    n_chunks = vt_ref.shape[0]
    acc_ref[...] = jnp.zeros(acc_ref.shape, F32)

    def chunk(c, carry):
        start = pl.multiple_of(c * tk, tk)
        k_c = k_ref[pl.ds(start, tk), :]
        vt_c = vt_ref[c]
        def scores(h):
            q_h = q_ref[:, h * HEAD_DIM:(h + 1) * HEAD_DIM]
            return lax.dot_general(k_c, q_h, (((1,), (1,)), ((), ())), preferred_element_type=F32)

        s_next = scores(0)
        for h in range(Q_PER_KV):
            s_t = s_next
            if h + 1 < Q_PER_KV:
                s_next = scores(h + 1)
            p_t = jnp.exp2(s_t).astype(BF16)
            acc_ref[h] += jnp.dot(vt_c, p_t, preferred_element_type=F32)
        return carry

    lax.fori_loop(0, n_chunks, chunk, 0)

    for h in range(Q_PER_KV):
        acc = acc_ref[h]
        o_t = acc[:HEAD_DIM] / acc[HEAD_DIM:HEAD_DIM + 1]
        o_ref[:, h * HEAD_DIM:(h + 1) * HEAD_DIM] = o_t.T.astype(o_ref.dtype)


def _attn_body_online(q_ref, k_ref, vt_ref, o_ref, m_ref, l_ref, acc_ref, *, tk):
    n_chunks = vt_ref.shape[0]
    m_ref[...] = jnp.full(m_ref.shape, -jnp.inf, F32)
    l_ref[...] = jnp.zeros(l_ref.shape, F32)
    acc_ref[...] = jnp.zeros(acc_ref.shape, F32)

    def chunk(c, carry):
        start = pl.multiple_of(c * tk, tk)
        k_c = k_ref[pl.ds(start, tk), :]
        vt_c = vt_ref[c][:HEAD_DIM]
        for h in range(Q_PER_KV):
            q_h = q_ref[:, h * HEAD_DIM:(h + 1) * HEAD_DIM]
            s_t = lax.dot_general(k_c, q_h, (((1,), (1,)), ((), ())), preferred_element_type=F32)
            m_old = m_ref[h]
            m_new = jnp.maximum(m_old, jnp.max(s_t, axis=0, keepdims=True))
            alpha = jnp.exp2(m_old - m_new)
            p_t = jnp.exp2(s_t - m_new)
            l_ref[h] = alpha * l_ref[h] + jnp.sum(p_t, axis=0, keepdims=True)
            acc_ref[h] = alpha * acc_ref[h] + jnp.dot(vt_c, p_t.astype(BF16), preferred_element_type=F32)
            m_ref[h] = m_new
        return carry

    lax.fori_loop(0, n_chunks, chunk, 0)

    for h in range(Q_PER_KV):
        o_t = acc_ref[h] / l_ref[h]
        o_ref[:, h * HEAD_DIM:(h + 1) * HEAD_DIM] = o_t.T.astype(o_ref.dtype)


def _attention(proj, vt_aug, *, tq, tk, bounded):
    s = proj.shape[0]
    group_w = Q_PER_KV * HEAD_DIM
    k_block0 = ATTN_WIDTH // HEAD_DIM
    est = (2 * tq * group_w * 2 * 2 + 2 * s * (HEAD_DIM + HEAD_AUG) * 2 + Q_PER_KV * (HEAD_AUG + 16) * tq * 4
           + 4 * tk * tq * 4)
    if bounded:
        body, name = _attn_body_bounded, "gqa_attention_bounded"
        scratch = [pltpu.VMEM((Q_PER_KV, HEAD_AUG, tq), F32)]
    else:
        body, name = _attn_body_online, "gqa_attention_online"
        scratch = [pltpu.VMEM((Q_PER_KV, 1, tq), F32), pltpu.VMEM((Q_PER_KV, 1, tq), F32),
                   pltpu.VMEM((Q_PER_KV, HEAD_DIM, tq), F32)]
    return pl.pallas_call(
        functools.partial(body, tk=tk),
        grid=(N_KV_HEADS, s // tq),
        in_specs=[
            pl.BlockSpec((tq, group_w), lambda g, i: (i, g)),
            pl.BlockSpec((s, HEAD_DIM), lambda g, i: (0, k_block0 + g)),
            pl.BlockSpec((None, s // tk, HEAD_AUG, tk), lambda g, i: (g, 0, 0, 0)),
        ],
        out_specs=pl.BlockSpec((tq, group_w), lambda g, i: (i, g)),
        out_shape=jax.ShapeDtypeStruct((s, ATTN_WIDTH), BF16),
        scratch_shapes=scratch,
        compiler_params=_params(est, 2),
        name=name,
    )(proj, proj, vt_aug)


def _score_bound(q_g, k_g):
    q_scale = (HEAD_DIM ** -0.5) * LOG2_E
    return HEAD_DIM * q_scale * jnp.max(jnp.abs(q_g)) * jnp.max(jnp.abs(k_g))


def _fourier1_body(f_ref, chan_ref, t_ref, y_ref, ab_ref, *, tb):
    n1 = f_ref.shape[0]
    for b in range(tb):
        for g in range(FOURIER_GROUPS):
            lo = b * FOURIER_WIDTH + g * FOURIER_GROUP_DIM
            ab = jnp.dot(f_ref[:, lo:lo + FOURIER_GROUP_DIM], chan_ref[...], preferred_element_type=F32)
            cols = slice(g * FOURIER_GROUP_DIM, (g + 1) * FOURIER_GROUP_DIM)
            ab_ref[:n1, cols] = ab[:, :FOURIER_GROUP_DIM].astype(ab_ref.dtype)
            ab_ref[n1:, cols] = ab[:, FOURIER_GROUP_DIM:].astype(ab_ref.dtype)
        y = jnp.dot(t_ref[b], ab_ref[...], preferred_element_type=F32)
        y_ref[0, b] = y[:n1].astype(y_ref.dtype)
        y_ref[1, b] = y[n1:].astype(y_ref.dtype)


def _fourier2_body(y_ref, f2_ref, o_ref, *, scale):
    two, n2, w = y_ref.shape
    y = y_ref[...].reshape(two * n2, w)
    o_ref[...] = (jnp.dot(f2_ref[...], y, preferred_element_type=F32) * scale).astype(o_ref.dtype)


def _dft_tables(seq):
    n2 = SEQ_DFT_MINOR
    n1 = seq // n2
    idx = jnp.arange(n2, dtype=jnp.int32)
    ang = (2.0 * math.pi / n2) * ((idx[:, None] * idx[None, :]) % n2).astype(F32)
    cm, sm = jnp.cos(ang), jnp.sin(ang)
    chan = jnp.concatenate([cm, sm], axis=1).astype(BF16)
    minor = jnp.concatenate([cm, sm], axis=1).astype(BF16)
    k1 = jnp.arange(n1, dtype=jnp.int32)[None, :, None]
    nn1 = jnp.arange(n1, dtype=jnp.int32)[None, None, :]
    nn2 = jnp.arange(n2, dtype=jnp.int32)[:, None, None]
    r = (k1 * (n2 * nn1 + nn2)) % seq
    a = (2.0 * math.pi / seq) * r.astype(F32)
    tc, ts = jnp.cos(a), jnp.sin(a)
    top = jnp.concatenate([tc, -ts], axis=2)
    bot = jnp.concatenate([-ts, -tc], axis=2)
    stage1 = jnp.concatenate([top, bot], axis=1).astype(BF16)
    return chan, stage1, minor


def _fourier(f, chan, stage1, minor, *, tb):
    s, w = f.shape
    n2 = SEQ_DFT_MINOR
    n1 = s // n2
    f_view = f.reshape(n1, n2 * w)
    est1 = 2 * n1 * tb * w * 2 + 2 * tb * 4 * n1 * n1 * 2 + 2 * 2 * tb * n1 * w * 2 + 2 * n1 * w * 2 + 4 * n1 * w * 4
    y = pl.pallas_call(
        functools.partial(_fourier1_body, tb=tb),
        grid=(n2 // tb,),
        in_specs=[
            pl.BlockSpec((n1, tb * w), lambda j: (0, j)),
            pl.BlockSpec((FOURIER_GROUP_DIM, 2 * FOURIER_GROUP_DIM), lambda j: (0, 0)),
            pl.BlockSpec((tb, 2 * n1, 2 * n1), lambda j: (j, 0, 0)),
        ],
        out_specs=pl.BlockSpec((2, tb, n1, w), lambda j: (0, j, 0, 0)),
        out_shape=jax.ShapeDtypeStruct((2, n2, n1, w), BF16),
        scratch_shapes=[pltpu.VMEM((2 * n1, w), BF16)],
        compiler_params=_params(est1, 1),
        name="fourier_stage1",
    )(f_view, chan, stage1)

    tk1 = tb
    y_view = y.reshape(2, n2, n1 * w)
    scale = 1.0 / math.sqrt(float(s) * FOURIER_GROUP_DIM)
    est2 = 2 * 2 * n2 * tk1 * w * 2 + 2 * n2 * tk1 * w * 2 + 2 * n2 * tk1 * w * 4
    out = pl.pallas_call(
        functools.partial(_fourier2_body, scale=scale),
        grid=(n1 // tk1,),
        in_specs=[
            pl.BlockSpec((2, n2, tk1 * w), lambda j: (0, 0, j)),
            pl.BlockSpec((n2, 2 * n2), lambda j: (0, 0)),
        ],
        out_specs=pl.BlockSpec((n2, tk1 * w), lambda j: (0, j)),
        out_shape=jax.ShapeDtypeStruct((n2, n1 * w), BF16),
        compiler_params=_params(est2, 1),
        name="fourier_stage2",
    )(y_view, minor)
    return out.reshape(s, w)


def _merge_body(a_ref, f_ref, ga_ref, gf_ref, h_ref, wa_ref, wf_ref, wo_ref, post_g_ref, o_ref):
    ya = jnp.dot(a_ref[...], wa_ref[...], preferred_element_type=F32)
    yf = jnp.dot(f_ref[...], wf_ref[...], preferred_element_type=F32)
    merged = ga_ref[...].astype(F32) * ya + gf_ref[...].astype(F32) * yf
    z = jnp.dot(merged.astype(BF16), wo_ref[...], preferred_element_type=F32)
    o_ref[...] = h_ref[...] + _rms(z, post_g_ref[...])


def _merge(attn, mixed, proj, h, w_attn_o, w_fourier, w_out, post_g, *, tm):
    s, d = h.shape
    gate_block0 = (ATTN_WIDTH + 2 * KV_WIDTH + FOURIER_WIDTH) // d
    resident = pl.Buffered(1)
    est = ((ATTN_WIDTH + FOURIER_WIDTH + d) * d * 2
           + 2 * tm * (ATTN_WIDTH + FOURIER_WIDTH + 2 * d) * 2 + 2 * 2 * tm * d * 4 + 4 * tm * d * 4)
    return pl.pallas_call(
        _merge_body,
        grid=(s // tm,),
        in_specs=[
            pl.BlockSpec((tm, ATTN_WIDTH), lambda i: (i, 0)),
            pl.BlockSpec((tm, FOURIER_WIDTH), lambda i: (i, 0)),
            pl.BlockSpec((tm, d), lambda i: (i, gate_block0)),
            pl.BlockSpec((tm, d), lambda i: (i, gate_block0 + 1)),
            pl.BlockSpec((tm, d), lambda i: (i, 0)),
            pl.BlockSpec((ATTN_WIDTH, d), lambda i: (0, 0), pipeline_mode=resident),
            pl.BlockSpec((FOURIER_WIDTH, d), lambda i: (0, 0), pipeline_mode=resident),
            pl.BlockSpec((d, d), lambda i: (0, 0), pipeline_mode=resident),
            pl.BlockSpec((1, d), lambda i: (0, 0)),
        ],
        out_specs=pl.BlockSpec((tm, d), lambda i: (i, 0)),
        out_shape=jax.ShapeDtypeStruct((s, d), F32),
        compiler_params=_params(est, 1),
        name="gated_merge",
    )(attn, mixed, proj, proj, h, w_attn_o, w_fourier, w_out, post_g)


def _tiles(seq):
    return dict(
        ffn_tm=min(512, seq), ffn_tf=512,
        proj_tm=min(1024, seq), proj_tn=1024,
        attn_tq=min(256, seq), attn_tk=min(4096, seq),
        four_tb=8,
        merge_tm=min(256, seq),
    )


def kernel(x, ffn1_pre_g, ffn1_w_gate, ffn1_w_up, ffn1_w_down, ffn1_post_g, mix_pre_g, w_in, b_gate, q_norm_g, k_norm_g, w_attn_o, w_fourier, w_out, mix_post_g, ffn2_pre_g, ffn2_w_gate, ffn2_w_up, ffn2_w_down, ffn2_post_g):
    batch, seq, d = x.shape
    assert batch == 1 and seq % SEQ_DFT_MINOR == 0
    depth = w_in.shape[0]
    t = _tiles(seq)
    cos_tab, sin_tab = _rope_tables(seq)
    chan, stage1, minor = _dft_tables(seq)
    row = lambda v: v.reshape(1, -1).astype(F32)

    h = x.reshape(seq, d)
    for l in range(depth):
        h = _ffn(h, row(ffn1_pre_g[l]), ffn1_w_gate[l].astype(BF16), ffn1_w_up[l].astype(BF16),
                 ffn1_w_down[l].astype(BF16), row(ffn1_post_g[l]), tm=t["ffn_tm"], tf=t["ffn_tf"])

        n_in = w_in.shape[2]
        bias_full = jnp.concatenate([jnp.zeros((1, n_in - 2 * d), F32), row(b_gate[l])], axis=1)
        proj = _in_proj(h, row(mix_pre_g[l]), w_in[l].astype(BF16), bias_full, row(q_norm_g[l]), row(k_norm_g[l]),
                        cos_tab, sin_tab, tm=t["proj_tm"], tn=t["proj_tn"])

        tk = t["attn_tk"]
        v = proj[:, ATTN_WIDTH + KV_WIDTH:ATTN_WIDTH + 2 * KV_WIDTH]
        vt = v.reshape(seq // tk, tk, N_KV_HEADS, HEAD_DIM).transpose(2, 0, 3, 1)
        pad_rows = jnp.zeros(vt.shape[:2] + (HEAD_AUG - HEAD_DIM, tk), BF16).at[:, :, 0].set(1.0)
        vt_aug = jnp.concatenate([vt, pad_rows], axis=2)
        attn = lax.cond(
            _score_bound(q_norm_g[l], k_norm_g[l]) <= SCORE_BOUND_LIMIT,
            functools.partial(_attention, tq=t["attn_tq"], tk=tk, bounded=True),
            functools.partial(_attention, tq=t["attn_tq"], tk=tk, bounded=False),
            proj, vt_aug)

        f_col0 = ATTN_WIDTH + 2 * KV_WIDTH
        mixed = _fourier(proj[:, f_col0:f_col0 + FOURIER_WIDTH], chan, stage1, minor, tb=t["four_tb"])

        h = _merge(attn, mixed, proj, h, w_attn_o[l].astype(BF16), w_fourier[l].astype(BF16),
                   w_out[l].astype(BF16), row(mix_post_g[l]), tm=t["merge_tm"])

        h = _ffn(h, row(ffn2_pre_g[l]), ffn2_w_gate[l].astype(BF16), ffn2_w_up[l].astype(BF16),
                 ffn2_w_down[l].astype(BF16), row(ffn2_post_g[l]), tm=t["ffn_tm"], tf=t["ffn_tf"])
    return h.reshape(batch, seq, d)
```

```python
import functools
import math

import jax
import jax.numpy as jnp
from jax import lax
from jax.experimental import pallas as pl
from jax.experimental.pallas import tpu as pltpu

HEAD_DIM = 128
N_KV_HEADS = 4
Q_PER_KV = 4
N_Q_HEADS = N_KV_HEADS * Q_PER_KV
ATTN_WIDTH = N_Q_HEADS * HEAD_DIM
KV_WIDTH = N_KV_HEADS * HEAD_DIM
FOURIER_GROUPS = 8
FOURIER_GROUP_DIM = 128
FOURIER_WIDTH = FOURIER_GROUPS * FOURIER_GROUP_DIM
ROPE_HALF = HEAD_DIM // 4
ROPE_THETA = 10000.0
GRID_W = 64
EPS = 1e-6
MACARON_WEIGHT = 0.5
LOG2_E = 1.4426950408889634

V7X_VMEM_BYTES = 64 * 1024 * 1024
V7X_LANES = 128
SEQ_DFT_MINOR = 128

BF16 = jnp.bfloat16
F32 = jnp.float32


def _vmem_limit(estimate_bytes):
    return int(min(estimate_bytes * 5 // 4 + (4 << 20), V7X_VMEM_BYTES - (6 << 20)))


def _params(estimate_bytes, n_axes):
    return pltpu.CompilerParams(
        dimension_semantics=("arbitrary",) * n_axes,
        vmem_limit_bytes=_vmem_limit(estimate_bytes),
    )


def _rms(x, g):
    ms = jnp.mean(x * x, axis=-1, keepdims=True)
    return x * lax.rsqrt(ms + EPS) * g


def _ffn_body(x_ref, pre_g_ref, wg_ref, wu_ref, wd_ref, post_g_ref, o_ref, xn_ref):
    j = pl.program_id(1)

    @pl.when(j == 0)
    def _():
        xn_ref[...] = _rms(x_ref[...], pre_g_ref[...]).astype(xn_ref.dtype)
        o_ref[...] = jnp.zeros(o_ref.shape, o_ref.dtype)

    xn = xn_ref[...]
    half = wg_ref.shape[1] // 2
    hidden = []
    for lo in (0, half):
        gate = jnp.dot(xn, wg_ref[:, lo:lo + half], preferred_element_type=F32)
        up = jnp.dot(xn, wu_ref[:, lo:lo + half], preferred_element_type=F32)
        hidden.append((gate * jax.nn.sigmoid(gate) * up).astype(BF16))
    o_ref[...] += jnp.dot(jnp.concatenate(hidden, axis=1), wd_ref[...], preferred_element_type=F32)

    @pl.when(j == pl.num_programs(1) - 1)
    def _():
        o_ref[...] = x_ref[...] + MACARON_WEIGHT * _rms(o_ref[...], post_g_ref[...])


def _ffn(x, pre_g, w_gate, w_up, w_down, post_g, *, tm, tf):
    s, d = x.shape
    d_ff = w_gate.shape[1]
    est = 2 * tm * d * 4 * 2 + tm * d * 2 + 2 * 3 * d * tf * 2 + 3 * tm * tf * 4 + tm * d * 4
    return pl.pallas_call(
        _ffn_body,
        grid=(s // tm, d_ff // tf),
        in_specs=[
            pl.BlockSpec((tm, d), lambda i, j: (i, 0)),
            pl.BlockSpec((1, d), lambda i, j: (0, 0)),
            pl.BlockSpec((d, tf), lambda i, j: (0, j)),
            pl.BlockSpec((d, tf), lambda i, j: (0, j)),
            pl.BlockSpec((tf, d), lambda i, j: (j, 0)),
            pl.BlockSpec((1, d), lambda i, j: (0, 0)),
        ],
        out_specs=pl.BlockSpec((tm, d), lambda i, j: (i, 0)),
        out_shape=jax.ShapeDtypeStruct((s, d), F32),
        scratch_shapes=[pltpu.VMEM((tm, d), BF16)],
        compiler_params=_params(est, 2),
        name="macaron_ffn",
    )(x, pre_g, w_gate, w_up, w_down, post_g)


PROJ_PIECE = 2 * HEAD_DIM


def _norm_rope_pair(y, head_mean, g2, cos, sin):
    ms = jnp.dot((y * y).astype(BF16), head_mean, preferred_element_type=F32)
    yn = y * lax.rsqrt(ms + EPS) * g2
    outs = []
    for h in range(2):
        yh = yn[:, h * HEAD_DIM:(h + 1) * HEAD_DIM]
        outs.append(yh * cos + pltpu.roll(yh, HEAD_DIM // 2, 1) * sin)
    return outs


def _inproj_body(x_ref, pre_g_ref, w_ref, bias_ref, qg_ref, kg_ref, cos_ref, sin_ref, mean_ref,
                 q_ref, k_ref, vt_ref, f_ref, gate_ref, xn_ref, *, tn):
    j = pl.program_id(1)
    n_q_tiles = ATTN_WIDTH // tn
    kv_tile = n_q_tiles
    f_tile = kv_tile + 1

    @pl.when(j == 0)
    def _():
        xn_ref[...] = _rms(x_ref[...], pre_g_ref[...]).astype(xn_ref.dtype)

    def piece(lo, width=PROJ_PIECE):
        return jnp.dot(xn_ref[...], w_ref[:, lo:lo + width], preferred_element_type=F32)

    @pl.when(j < n_q_tiles)
    def _():
        ys = [piece(lo) for lo in range(0, tn, PROJ_PIECE)]
        for p, y in enumerate(ys):
            outs = _norm_rope_pair(y, mean_ref[...], qg_ref[...], cos_ref[...], sin_ref[...])
            for h, out in enumerate(outs):
                col = p * PROJ_PIECE + h * HEAD_DIM
                q_ref[:, col:col + HEAD_DIM] = out.astype(q_ref.dtype)

    @pl.when(j == kv_tile)
    def _():
        ys = [piece(lo) for lo in range(0, KV_WIDTH, PROJ_PIECE)]
        for p, y in enumerate(ys):
            outs = _norm_rope_pair(y, mean_ref[...], kg_ref[...], cos_ref[...], sin_ref[...])
            for h, out in enumerate(outs):
                k_ref[p * (PROJ_PIECE // HEAD_DIM) + h] = out.astype(k_ref.dtype)
        for lo in range(KV_WIDTH, 2 * KV_WIDTH, PROJ_PIECE):
            vt_ref[lo - KV_WIDTH:lo - KV_WIDTH + PROJ_PIECE, :] = piece(lo).astype(vt_ref.dtype).T

    @pl.when(j == f_tile)
    def _():
        for lo in range(0, tn, PROJ_PIECE):
            f_ref[:, lo:lo + PROJ_PIECE] = piece(lo).astype(f_ref.dtype)

    @pl.when(j > f_tile)
    def _():
        for lo in range(0, tn, PROJ_PIECE):
            gate_ref[:, lo:lo + PROJ_PIECE] = jax.nn.sigmoid(
                piece(lo) + bias_ref[:, lo:lo + PROJ_PIECE]).astype(gate_ref.dtype)


def _in_proj(h, pre_g, w_in, bias_full, q_g2, k_g2, cos_tab, sin_tab, *, tm, tn):
    s, d = h.shape
    n = w_in.shape[1]
    assert tn == 2 * KV_WIDTH == FOURIER_WIDTH and ATTN_WIDTH % tn == 0
    n_q_tiles = ATTN_WIDTH // tn
    gate_tile0 = n_q_tiles + 2
    n_gate_tiles = n // tn - gate_tile0
    lane_head = jnp.arange(PROJ_PIECE, dtype=jnp.int32) // HEAD_DIM
    head_mean = jnp.where(lane_head[:, None] == lane_head[None, :], 1.0 / HEAD_DIM, 0.0).astype(BF16)
    est = (2 * tm * d * 4 + tm * d * 2 + 2 * d * tn * 2 + 2 * 2 * tm * tn * 2 * 2 + 2 * 2 * tm * KV_WIDTH * 2
           + 4 * tm * PROJ_PIECE * 4 + 4 * tm * HEAD_DIM * 4)
    return pl.pallas_call(
        functools.partial(_inproj_body, tn=tn),
        grid=(s // tm, n // tn),
        in_specs=[
            pl.BlockSpec((tm, d), lambda i, j: (i, 0)),
            pl.BlockSpec((1, d), lambda i, j: (0, 0)),
            pl.BlockSpec((d, tn), lambda i, j: (0, j)),
            pl.BlockSpec((1, tn), lambda i, j: (0, j)),
            pl.BlockSpec((1, PROJ_PIECE), lambda i, j: (0, 0)),
            pl.BlockSpec((1, PROJ_PIECE), lambda i, j: (0, 0)),
            pl.BlockSpec((tm, HEAD_DIM), lambda i, j: (i, 0)),
            pl.BlockSpec((tm, HEAD_DIM), lambda i, j: (i, 0)),
            pl.BlockSpec((PROJ_PIECE, PROJ_PIECE), lambda i, j: (0, 0)),
        ],
        out_specs=[
            pl.BlockSpec((tm, tn), lambda i, j: (i, jnp.minimum(j, n_q_tiles - 1))),
            pl.BlockSpec((N_KV_HEADS, tm, HEAD_DIM), lambda i, j: (0, i, 0)),
            pl.BlockSpec((KV_WIDTH, tm), lambda i, j: (0, i)),
            pl.BlockSpec((tm, FOURIER_WIDTH), lambda i, j: (i, 0)),
            pl.BlockSpec((tm, tn), lambda i, j: (i, jnp.clip(j - gate_tile0, 0, n_gate_tiles - 1))),
        ],
        out_shape=[
            jax.ShapeDtypeStruct((s, ATTN_WIDTH), BF16),
            jax.ShapeDtypeStruct((N_KV_HEADS, s, HEAD_DIM), BF16),
            jax.ShapeDtypeStruct((KV_WIDTH, s), BF16),
            jax.ShapeDtypeStruct((s, FOURIER_WIDTH), BF16),
            jax.ShapeDtypeStruct((s, n_gate_tiles * tn), BF16),
        ],
        scratch_shapes=[pltpu.VMEM((tm, d), BF16)],
        compiler_params=_params(est, 2),
        name="in_proj",
    )(h, pre_g, w_in, bias_full, q_g2, k_g2, cos_tab, sin_tab, head_mean)


def _rope_head_perm(v, n_heads):
    lead = v.shape[:-1]
    v = v.reshape(lead + (n_heads, 2, 2, ROPE_HALF))
    return jnp.swapaxes(v, -3, -2).reshape(lead + (n_heads * HEAD_DIM,))


def _rope_tables(seq):
    t = jnp.arange(seq, dtype=jnp.int32)
    row = (t // GRID_W).astype(F32)
    col = (t % GRID_W).astype(F32)
    n_freq = ROPE_HALF
    inv_freq = ROPE_THETA ** (-jnp.arange(n_freq, dtype=F32) / n_freq)
    ang_r = row[:, None] * inv_freq
    ang_c = col[:, None] * inv_freq
    cos_tab = jnp.concatenate([jnp.cos(ang_r), jnp.cos(ang_c), jnp.cos(ang_r), jnp.cos(ang_c)], axis=-1)
    sin_tab = jnp.concatenate([-jnp.sin(ang_r), -jnp.sin(ang_c), jnp.sin(ang_r), jnp.sin(ang_c)], axis=-1)
    return cos_tab, sin_tab


SCORE_BOUND_LIMIT = 100.0


def _kv_chunks(k_ref, vt_ref, tk):
    for c in range(k_ref.shape[0] // tk):
        yield k_ref[c * tk:(c + 1) * tk, :], vt_ref[:, c * tk:(c + 1) * tk]


def _scores_t(k_c, q_ref, h):
    q_h = q_ref[:, h * HEAD_DIM:(h + 1) * HEAD_DIM]
    return lax.dot_general(k_c, q_h, (((1,), (1,)), ((), ())), preferred_element_type=F32)


def _attn_finish(o_ref, l_ref, acc_ref):
    for h in range(Q_PER_KV):
        o_t = acc_ref[h] / l_ref[h]
        o_ref[:, h * HEAD_DIM:(h + 1) * HEAD_DIM] = o_t.T.astype(o_ref.dtype)


def _attn_body_bounded(q_ref, k_ref, vt_ref, o_ref, l_ref, acc_ref, *, tk):
    l_ref[...] = jnp.zeros(l_ref.shape, F32)
    acc_ref[...] = jnp.zeros(acc_ref.shape, F32)
    for k_c, vt_c in _kv_chunks(k_ref, vt_ref, tk):
        s_next = _scores_t(k_c, q_ref, 0)
        for h in range(Q_PER_KV):
            s_t = s_next
            if h + 1 < Q_PER_KV:
                s_next = _scores_t(k_c, q_ref, h + 1)
            p_t = jnp.exp2(s_t)
            l_ref[h] += jnp.sum(p_t, axis=0, keepdims=True)
            acc_ref[h] += jnp.dot(vt_c, p_t.astype(BF16), preferred_element_type=F32)
    _attn_finish(o_ref, l_ref, acc_ref)


def _attn_body_online(q_ref, k_ref, vt_ref, o_ref, m_ref, l_ref, acc_ref, *, tk):
    m_ref[...] = jnp.full(m_ref.shape, -jnp.inf, F32)
    l_ref[...] = jnp.zeros(l_ref.shape, F32)
    acc_ref[...] = jnp.zeros(acc_ref.shape, F32)
    for k_c, vt_c in _kv_chunks(k_ref, vt_ref, tk):
        for h in range(Q_PER_KV):
            s_t = _scores_t(k_c, q_ref, h)
            m_old = m_ref[h]
            m_new = jnp.maximum(m_old, jnp.max(s_t, axis=0, keepdims=True))
            alpha = jnp.exp2(m_old - m_new)
            p_t = jnp.exp2(s_t - m_new)
            l_ref[h] = alpha * l_ref[h] + jnp.sum(p_t, axis=0, keepdims=True)
            acc_ref[h] = alpha * acc_ref[h] + jnp.dot(vt_c, p_t.astype(BF16), preferred_element_type=F32)
            m_ref[h] = m_new
    _attn_finish(o_ref, l_ref, acc_ref)


def _attention(q, k, vt, *, tq, tk, bounded):
    s = q.shape[0]
    group_w = Q_PER_KV * HEAD_DIM
    est = (2 * tq * group_w * 2 * 2 + 2 * s * 2 * HEAD_DIM * 2 + Q_PER_KV * (HEAD_DIM + 16) * tq * 4
           + 4 * tk * tq * 4)
    stat = pltpu.VMEM((Q_PER_KV, 1, tq), F32)
    acc = pltpu.VMEM((Q_PER_KV, HEAD_DIM, tq), F32)
    if bounded:
        body, name, scratch = _attn_body_bounded, "gqa_attention_bounded", [stat, acc]
    else:
        body, name, scratch = _attn_body_online, "gqa_attention_online", [stat, stat, acc]
    return pl.pallas_call(
        functools.partial(body, tk=tk),
        grid=(N_KV_HEADS, s // tq),
        in_specs=[
            pl.BlockSpec((tq, group_w), lambda g, i: (i, g)),
            pl.BlockSpec((None, s, HEAD_DIM), lambda g, i: (g, 0, 0)),
            pl.BlockSpec((HEAD_DIM, s), lambda g, i: (g, 0)),
        ],
        out_specs=pl.BlockSpec((tq, group_w), lambda g, i: (i, g)),
        out_shape=jax.ShapeDtypeStruct((s, ATTN_WIDTH), BF16),
        scratch_shapes=scratch,
        compiler_params=_params(est, 2),
        name=name,
    )(q, k, vt)


def _score_bound(q_g, k_g):
    q_scale = (HEAD_DIM ** -0.5) * LOG2_E
    return HEAD_DIM * q_scale * jnp.max(jnp.abs(q_g)) * jnp.max(jnp.abs(k_g))


def _fourier1_body(f_ref, chan_ref, t_ref, y_ref, ab_ref, *, tb):
    n1 = f_ref.shape[0]
    for b in range(tb):
        for g in range(FOURIER_GROUPS):
            lo = b * FOURIER_WIDTH + g * FOURIER_GROUP_DIM
            ab = jnp.dot(f_ref[:, lo:lo + FOURIER_GROUP_DIM], chan_ref[...], preferred_element_type=F32)
            cols = slice(g * FOURIER_GROUP_DIM, (g + 1) * FOURIER_GROUP_DIM)
            ab_ref[:n1, cols] = ab[:, :FOURIER_GROUP_DIM].astype(ab_ref.dtype)
            ab_ref[n1:, cols] = ab[:, FOURIER_GROUP_DIM:].astype(ab_ref.dtype)
        y = jnp.dot(t_ref[b], ab_ref[...], preferred_element_type=F32)
        y_ref[0, b] = y[:n1].astype(y_ref.dtype)
        y_ref[1, b] = y[n1:].astype(y_ref.dtype)


def _fourier2_body(y_ref, f2_ref, o_ref, *, scale):
    two, n2, w = y_ref.shape
    y = y_ref[...].reshape(two * n2, w)
    o_ref[...] = (jnp.dot(f2_ref[...], y, preferred_element_type=F32) * scale).astype(o_ref.dtype)


def _dft_tables(seq):
    n2 = SEQ_DFT_MINOR
    n1 = seq // n2
    idx = jnp.arange(n2, dtype=jnp.int32)
    ang = (2.0 * math.pi / n2) * ((idx[:, None] * idx[None, :]) % n2).astype(F32)
    cm, sm = jnp.cos(ang), jnp.sin(ang)
    chan = jnp.concatenate([cm, sm], axis=1).astype(BF16)
    minor = jnp.concatenate([cm, sm], axis=1).astype(BF16)
    k1 = jnp.arange(n1, dtype=jnp.int32)[None, :, None]
    nn1 = jnp.arange(n1, dtype=jnp.int32)[None, None, :]
    nn2 = jnp.arange(n2, dtype=jnp.int32)[:, None, None]
    r = (k1 * (n2 * nn1 + nn2)) % seq
    a = (2.0 * math.pi / seq) * r.astype(F32)
    tc, ts = jnp.cos(a), jnp.sin(a)
    top = jnp.concatenate([tc, -ts], axis=2)
    bot = jnp.concatenate([-ts, -tc], axis=2)
    stage1 = jnp.concatenate([top, bot], axis=1).astype(BF16)
    return chan, stage1, minor


def _fourier(f, chan, stage1, minor, *, tb):
    s, w = f.shape
    n2 = SEQ_DFT_MINOR
    n1 = s // n2
    f_view = f.reshape(n1, n2 * w)
    est1 = 2 * n1 * tb * w * 2 + 2 * tb * 4 * n1 * n1 * 2 + 2 * 2 * tb * n1 * w * 2 + 2 * n1 * w * 2 + 4 * n1 * w * 4
    y = pl.pallas_call(
        functools.partial(_fourier1_body, tb=tb),
        grid=(n2 // tb,),
        in_specs=[
            pl.BlockSpec((n1, tb * w), lambda j: (0, j)),
            pl.BlockSpec((FOURIER_GROUP_DIM, 2 * FOURIER_GROUP_DIM), lambda j: (0, 0)),
            pl.BlockSpec((tb, 2 * n1, 2 * n1), lambda j: (j, 0, 0)),
        ],
        out_specs=pl.BlockSpec((2, tb, n1, w), lambda j: (0, j, 0, 0)),
        out_shape=jax.ShapeDtypeStruct((2, n2, n1, w), BF16),
        scratch_shapes=[pltpu.VMEM((2 * n1, w), BF16)],
        compiler_params=_params(est1, 1),
        name="fourier_stage1",
    )(f_view, chan, stage1)

    tk1 = tb
    y_view = y.reshape(2, n2, n1 * w)
    scale = 1.0 / math.sqrt(float(s) * FOURIER_GROUP_DIM)
    est2 = 2 * 2 * n2 * tk1 * w * 2 + 2 * n2 * tk1 * w * 2 + 2 * n2 * tk1 * w * 4
    out = pl.pallas_call(
        functools.partial(_fourier2_body, scale=scale),
        grid=(n1 // tk1,),
        in_specs=[
            pl.BlockSpec((2, n2, tk1 * w), lambda j: (0, 0, j)),
            pl.BlockSpec((n2, 2 * n2), lambda j: (0, 0)),
        ],
        out_specs=pl.BlockSpec((n2, tk1 * w), lambda j: (0, j)),
        out_shape=jax.ShapeDtypeStruct((n2, n1 * w), BF16),
        compiler_params=_params(est2, 1),
        name="fourier_stage2",
    )(y_view, minor)
    return out.reshape(s, w)


def _merge_body(a_ref, f_ref, ga_ref, gf_ref, h_ref, wa_ref, wf_ref, wo_ref, post_g_ref, o_ref):
    ya = jnp.dot(a_ref[...], wa_ref[...], preferred_element_type=F32)
    yf = jnp.dot(f_ref[...], wf_ref[...], preferred_element_type=F32)
    merged = ga_ref[...].astype(F32) * ya + gf_ref[...].astype(F32) * yf
    z = jnp.dot(merged.astype(BF16), wo_ref[...], preferred_element_type=F32)
    o_ref[...] = h_ref[...] + _rms(z, post_g_ref[...])


def _merge(attn, mixed, gates, h, w_attn_o, w_fourier, w_out, post_g, *, tm):
    s, d = h.shape
    resident = pl.Buffered(1)
    est = ((ATTN_WIDTH + FOURIER_WIDTH + d) * d * 2
           + 2 * tm * (ATTN_WIDTH + FOURIER_WIDTH + 2 * d) * 2 + 2 * 2 * tm * d * 4 + 4 * tm * d * 4)
    return pl.pallas_call(
        _merge_body,
        grid=(s // tm,),
        in_specs=[
            pl.BlockSpec((tm, ATTN_WIDTH), lambda i: (i, 0)),
            pl.BlockSpec((tm, FOURIER_WIDTH), lambda i: (i, 0)),
            pl.BlockSpec((tm, d), lambda i: (i, 0)),
            pl.BlockSpec((tm, d), lambda i: (i, 1)),
            pl.BlockSpec((tm, d), lambda i: (i, 0)),
            pl.BlockSpec((ATTN_WIDTH, d), lambda i: (0, 0), pipeline_mode=resident),
            pl.BlockSpec((FOURIER_WIDTH, d), lambda i: (0, 0), pipeline_mode=resident),
            pl.BlockSpec((d, d), lambda i: (0, 0), pipeline_mode=resident),
            pl.BlockSpec((1, d), lambda i: (0, 0)),
        ],
        out_specs=pl.BlockSpec((tm, d), lambda i: (i, 0)),
        out_shape=jax.ShapeDtypeStruct((s, d), F32),
        compiler_params=_params(est, 1),
        name="gated_merge",
    )(attn, mixed, gates, gates, h, w_attn_o, w_fourier, w_out, post_g)


def _tiles(seq):
    return dict(
        ffn_tm=min(512, seq), ffn_tf=512,
        proj_tm=min(1024, seq), proj_tn=1024,
        attn_tq=min(256, seq), attn_tk=min(8192, seq),
        four_tb=8,
        merge_tm=min(256, seq),
    )


def kernel(x, ffn1_pre_g, ffn1_w_gate, ffn1_w_up, ffn1_w_down, ffn1_post_g, mix_pre_g, w_in, b_gate, q_norm_g, k_norm_g, w_attn_o, w_fourier, w_out, mix_post_g, ffn2_pre_g, ffn2_w_gate, ffn2_w_up, ffn2_w_down, ffn2_post_g):
    batch, seq, d = x.shape
    assert batch == 1 and seq % SEQ_DFT_MINOR == 0
    depth = w_in.shape[0]
    t = _tiles(seq)
    cos_tab, sin_tab = _rope_tables(seq)
    chan, stage1, minor = _dft_tables(seq)
    row = lambda v: v.reshape(1, -1).astype(F32)

    h = x.reshape(seq, d)
    for l in range(depth):
        h = _ffn(h, row(ffn1_pre_g[l]), ffn1_w_gate[l].astype(BF16), ffn1_w_up[l].astype(BF16),
                 ffn1_w_down[l].astype(BF16), row(ffn1_post_g[l]), tm=t["ffn_tm"], tf=t["ffn_tf"])

        n_in = w_in.shape[2]
        bias_full = jnp.concatenate([jnp.zeros((1, n_in - 2 * d), F32), row(b_gate[l])], axis=1)
        n_qk = ATTN_WIDTH + KV_WIDTH
        w_proj = jnp.concatenate([_rope_head_perm(w_in[l][:, :n_qk], N_Q_HEADS + N_KV_HEADS), w_in[l][:, n_qk:]],
                                 axis=1).astype(BF16)
        q_scale = (HEAD_DIM ** -0.5) * LOG2_E
        q_g2 = jnp.tile(_rope_head_perm(q_norm_g[l].astype(F32), 1) * q_scale, 2).reshape(1, -1)
        k_g2 = jnp.tile(_rope_head_perm(k_norm_g[l].astype(F32), 1), 2).reshape(1, -1)
        q, k, vt, f, gates = _in_proj(h, row(mix_pre_g[l]), w_proj, bias_full, q_g2, k_g2, cos_tab, sin_tab,
                                      tm=t["proj_tm"], tn=t["proj_tn"])

        attn = lax.cond(
            _score_bound(q_norm_g[l], k_norm_g[l]) <= SCORE_BOUND_LIMIT,
            functools.partial(_attention, tq=t["attn_tq"], tk=t["attn_tk"], bounded=True),
            functools.partial(_attention, tq=t["attn_tq"], tk=t["attn_tk"], bounded=False),
            q, k, vt)

        mixed = _fourier(f, chan, stage1, minor, tb=t["four_tb"])

        h = _merge(attn, mixed, gates, h, w_attn_o[l].astype(BF16), w_fourier[l].astype(BF16),
                   w_out[l].astype(BF16), row(mix_post_g[l]), tm=t["merge_tm"])

        h = _ffn(h, row(ffn2_pre_g[l]), ffn2_w_gate[l].astype(BF16), ffn2_w_up[l].astype(BF16),
                 ffn2_w_down[l].astype(BF16), row(ffn2_post_g[l]), tm=t["ffn_tm"], tf=t["ffn_tf"])
    return h.reshape(batch, seq, d)
```

```python
import functools
import math

import jax
import jax.numpy as jnp
from jax import lax
from jax.experimental import pallas as pl
from jax.experimental.pallas import tpu as pltpu

HEAD_DIM = 128
N_KV_HEADS = 4
Q_PER_KV = 4
N_Q_HEADS = N_KV_HEADS * Q_PER_KV
ATTN_WIDTH = N_Q_HEADS * HEAD_DIM
KV_WIDTH = N_KV_HEADS * HEAD_DIM
FOURIER_GROUPS = 8
FOURIER_GROUP_DIM = 128
FOURIER_WIDTH = FOURIER_GROUPS * FOURIER_GROUP_DIM
ROPE_HALF = HEAD_DIM // 4
ROPE_THETA = 10000.0
GRID_W = 64
EPS = 1e-6
MACARON_WEIGHT = 0.5
LOG2_E = 1.4426950408889634

V7X_VMEM_BYTES = 64 * 1024 * 1024
V7X_LANES = 128
SEQ_DFT_MINOR = 128

BF16 = jnp.bfloat16
F32 = jnp.float32


def _vmem_limit(estimate_bytes):
    return int(min(estimate_bytes * 5 // 4 + (4 << 20), V7X_VMEM_BYTES - (6 << 20)))


def _params(estimate_bytes, n_axes):
    return pltpu.CompilerParams(
        dimension_semantics=("arbitrary",) * n_axes,
        vmem_limit_bytes=_vmem_limit(estimate_bytes),
    )


def _rms(x, g):
    ms = jnp.mean(x * x, axis=-1, keepdims=True)
    return x * lax.rsqrt(ms + EPS) * g


def _ffn_body(x_ref, pre_g_ref, wg_ref, wu_ref, wd_ref, post_g_ref, o_ref, xn_ref):
    j = pl.program_id(1)

    @pl.when(j == 0)
    def _():
        xn_ref[...] = _rms(x_ref[...], pre_g_ref[...]).astype(xn_ref.dtype)
        o_ref[...] = jnp.zeros(o_ref.shape, o_ref.dtype)

    xn = xn_ref[...]
    half = wg_ref.shape[1] // 2
    hidden = []
    for lo in (0, half):
        gate = jnp.dot(xn, wg_ref[:, lo:lo + half], preferred_element_type=F32)
        up = jnp.dot(xn, wu_ref[:, lo:lo + half], preferred_element_type=F32)
        hidden.append((gate * jax.nn.sigmoid(gate) * up).astype(BF16))
    o_ref[...] += jnp.dot(jnp.concatenate(hidden, axis=1), wd_ref[...], preferred_element_type=F32)

    @pl.when(j == pl.num_programs(1) - 1)
    def _():
        o_ref[...] = x_ref[...] + MACARON_WEIGHT * _rms(o_ref[...], post_g_ref[...])


def _ffn(x, pre_g, w_gate, w_up, w_down, post_g, *, tm, tf):
    s, d = x.shape
    d_ff = w_gate.shape[1]
    est = 2 * tm * d * 4 * 2 + tm * d * 2 + 2 * 3 * d * tf * 2 + 3 * tm * tf * 4 + tm * d * 4
    return pl.pallas_call(
        _ffn_body,
        grid=(s // tm, d_ff // tf),
        in_specs=[
            pl.BlockSpec((tm, d), lambda i, j: (i, 0)),
            pl.BlockSpec((1, d), lambda i, j: (0, 0)),
            pl.BlockSpec((d, tf), lambda i, j: (0, j)),
            pl.BlockSpec((d, tf), lambda i, j: (0, j)),
            pl.BlockSpec((tf, d), lambda i, j: (j, 0)),
            pl.BlockSpec((1, d), lambda i, j: (0, 0)),
        ],
        out_specs=pl.BlockSpec((tm, d), lambda i, j: (i, 0)),
        out_shape=jax.ShapeDtypeStruct((s, d), F32),
        scratch_shapes=[pltpu.VMEM((tm, d), BF16)],
        compiler_params=_params(est, 2),
        name="macaron_ffn",
    )(x, pre_g, w_gate, w_up, w_down, post_g)


PROJ_PIECE = 2 * HEAD_DIM


def _norm_rope_pair(y, head_mean, g2, cos, sin):
    ms = jnp.dot((y * y).astype(BF16), head_mean, preferred_element_type=F32)
    yn = y * lax.rsqrt(ms + EPS) * g2
    outs = []
    for h in range(2):
        yh = yn[:, h * HEAD_DIM:(h + 1) * HEAD_DIM]
        outs.append(yh * cos + pltpu.roll(yh, HEAD_DIM // 2, 1) * sin)
    return outs


def _inproj_body(x_ref, pre_g_ref, w_ref, bias_ref, qg_ref, kg_ref, cos_ref, sin_ref, mean_ref,
                 q_ref, k_ref, vt_ref, f_ref, gate_ref, xn_ref, *, tn):
    j = pl.program_id(1)
    n_q_tiles = ATTN_WIDTH // tn
    kv_tile = n_q_tiles
    f_tile = kv_tile + 1

    @pl.when(j == 0)
    def _():
        xn_ref[...] = _rms(x_ref[...], pre_g_ref[...]).astype(xn_ref.dtype)

    def piece(lo, width=PROJ_PIECE):
        return jnp.dot(xn_ref[...], w_ref[:, lo:lo + width], preferred_element_type=F32)

    def normed_heads(width, g_ref):
        for lo2 in range(0, width, 2 * PROJ_PIECE):
            ys = [piece(lo) for lo in range(lo2, lo2 + 2 * PROJ_PIECE, PROJ_PIECE)]
            for p, y in enumerate(ys):
                outs = _norm_rope_pair(y, mean_ref[...], g_ref[...], cos_ref[...], sin_ref[...])
                for h, out in enumerate(outs):
                    yield (lo2 + p * PROJ_PIECE) // HEAD_DIM + h, out

    @pl.when(j < n_q_tiles)
    def _():
        for head, out in normed_heads(tn, qg_ref):
            q_ref[:, head * HEAD_DIM:(head + 1) * HEAD_DIM] = out.astype(q_ref.dtype)

    @pl.when(j == kv_tile)
    def _():
        for head, out in normed_heads(KV_WIDTH, kg_ref):
            k_ref[head] = out.astype(k_ref.dtype)
        for lo in range(KV_WIDTH, 2 * KV_WIDTH, PROJ_PIECE):
            vt_ref[lo - KV_WIDTH:lo - KV_WIDTH + PROJ_PIECE, :] = piece(lo).astype(vt_ref.dtype).T

    @pl.when(j == f_tile)
    def _():
        for lo in range(0, tn, PROJ_PIECE):
            f_ref[:, lo:lo + PROJ_PIECE] = piece(lo).astype(f_ref.dtype)

    @pl.when(j > f_tile)
    def _():
        for lo in range(0, tn, PROJ_PIECE):
            z = piece(lo) + bias_ref[:, lo:lo + PROJ_PIECE]
            gate_ref[:, lo:lo + PROJ_PIECE] = (0.5 * jnp.tanh(0.5 * z) + 0.5).astype(gate_ref.dtype)


def _in_proj(h, pre_g, w_in, bias_full, q_g2, k_g2, cos_tab, sin_tab, *, tm, tn):
    s, d = h.shape
    n = w_in.shape[1]
    assert tn == 2 * KV_WIDTH == FOURIER_WIDTH and ATTN_WIDTH % tn == 0
    n_q_tiles = ATTN_WIDTH // tn
    gate_tile0 = n_q_tiles + 2
    n_gate_tiles = n // tn - gate_tile0
    lane_head = jnp.arange(PROJ_PIECE, dtype=jnp.int32) // HEAD_DIM
    head_mean = jnp.where(lane_head[:, None] == lane_head[None, :], 1.0 / HEAD_DIM, 0.0).astype(BF16)
    est = (2 * tm * d * 4 + tm * d * 2 + 2 * d * tn * 2 + 2 * 2 * tm * tn * 2 * 2 + 2 * 2 * tm * KV_WIDTH * 2
           + 4 * tm * PROJ_PIECE * 4 + 4 * tm * HEAD_DIM * 4)
    return pl.pallas_call(
        functools.partial(_inproj_body, tn=tn),
        grid=(s // tm, n // tn),
        in_specs=[
            pl.BlockSpec((tm, d), lambda i, j: (i, 0)),
            pl.BlockSpec((1, d), lambda i, j: (0, 0)),
            pl.BlockSpec((d, tn), lambda i, j: (0, j)),
            pl.BlockSpec((1, tn), lambda i, j: (0, j)),
            pl.BlockSpec((1, PROJ_PIECE), lambda i, j: (0, 0)),
            pl.BlockSpec((1, PROJ_PIECE), lambda i, j: (0, 0)),
            pl.BlockSpec((tm, HEAD_DIM), lambda i, j: (i, 0)),
            pl.BlockSpec((tm, HEAD_DIM), lambda i, j: (i, 0)),
            pl.BlockSpec((PROJ_PIECE, PROJ_PIECE), lambda i, j: (0, 0)),
        ],
        out_specs=[
            pl.BlockSpec((tm, tn), lambda i, j: (i, jnp.minimum(j, n_q_tiles - 1))),
            pl.BlockSpec((N_KV_HEADS, tm, HEAD_DIM), lambda i, j: (0, i, 0)),
            pl.BlockSpec((KV_WIDTH, tm), lambda i, j: (0, i)),
            pl.BlockSpec((tm, FOURIER_WIDTH), lambda i, j: (i, 0)),
            pl.BlockSpec((tm, tn), lambda i, j: (i, jnp.clip(j - gate_tile0, 0, n_gate_tiles - 1))),
        ],
        out_shape=[
            jax.ShapeDtypeStruct((s, ATTN_WIDTH), BF16),
            jax.ShapeDtypeStruct((N_KV_HEADS, s, HEAD_DIM), BF16),
            jax.ShapeDtypeStruct((KV_WIDTH, s), BF16),
            jax.ShapeDtypeStruct((s, FOURIER_WIDTH), BF16),
            jax.ShapeDtypeStruct((s, n_gate_tiles * tn), BF16),
        ],
        scratch_shapes=[pltpu.VMEM((tm, d), BF16)],
        compiler_params=_params(est, 2),
        name="in_proj",
    )(h, pre_g, w_in, bias_full, q_g2, k_g2, cos_tab, sin_tab, head_mean)


def _rope_head_perm(v, n_heads):
    lead = v.shape[:-1]
    v = v.reshape(lead + (n_heads, 2, 2, ROPE_HALF))
    return jnp.swapaxes(v, -3, -2).reshape(lead + (n_heads * HEAD_DIM,))


def _rope_tables(seq):
    n_rows = seq // GRID_W
    n_freq = ROPE_HALF
    inv_freq = ROPE_THETA ** (-jnp.arange(n_freq, dtype=F32) / n_freq)
    ang_r = jnp.arange(n_rows, dtype=F32)[:, None] * inv_freq
    ang_c = jnp.arange(GRID_W, dtype=F32)[:, None] * inv_freq
    by_row = lambda v: jnp.repeat(v, GRID_W, axis=0)
    by_col = lambda v: jnp.tile(v, (n_rows, 1))
    cos_r, sin_r = by_row(jnp.cos(ang_r)), by_row(jnp.sin(ang_r))
    cos_c, sin_c = by_col(jnp.cos(ang_c)), by_col(jnp.sin(ang_c))
    cos_tab = jnp.concatenate([cos_r, cos_c, cos_r, cos_c], axis=-1)
    sin_tab = jnp.concatenate([-sin_r, -sin_c, sin_r, sin_c], axis=-1)
    return cos_tab, sin_tab


SCORE_BOUND_LIMIT = 100.0


def _kv_chunks(k_ref, vt_ref, tk):
    for c in range(k_ref.shape[0] // tk):
        yield k_ref[c * tk:(c + 1) * tk, :], vt_ref[:, c * tk:(c + 1) * tk]


def _scores_t(k_c, q_ref, h):
    q_h = q_ref[:, h * HEAD_DIM:(h + 1) * HEAD_DIM]
    return lax.dot_general(k_c, q_h, (((1,), (1,)), ((), ())), preferred_element_type=F32)


def _attn_finish(o_ref, l_ref, acc_ref):
    for h in range(Q_PER_KV):
        o_t = acc_ref[h] / l_ref[h]
        o_ref[:, h * HEAD_DIM:(h + 1) * HEAD_DIM] = o_t.T.astype(o_ref.dtype)


def _attn_body_bounded(q_ref, k_ref, vt_ref, o_ref, l_ref, acc_ref, *, tk):
    l_ref[...] = jnp.zeros(l_ref.shape, F32)
    acc_ref[...] = jnp.zeros(acc_ref.shape, F32)
    for k_c, vt_c in _kv_chunks(k_ref, vt_ref, tk):
        s_next = _scores_t(k_c, q_ref, 0)
        for h in range(Q_PER_KV):
            s_t = s_next
            if h + 1 < Q_PER_KV:
                s_next = _scores_t(k_c, q_ref, h + 1)
            p_t = jnp.exp2(s_t)
            l_ref[h] += jnp.sum(p_t, axis=0, keepdims=True)
            acc_ref[h] += jnp.dot(vt_c, p_t.astype(BF16), preferred_element_type=F32)
    _attn_finish(o_ref, l_ref, acc_ref)


def _attn_body_online(q_ref, k_ref, vt_ref, o_ref, m_ref, l_ref, acc_ref, *, tk):
    m_ref[...] = jnp.full(m_ref.shape, -jnp.inf, F32)
    l_ref[...] = jnp.zeros(l_ref.shape, F32)
    acc_ref[...] = jnp.zeros(acc_ref.shape, F32)
    for k_c, vt_c in _kv_chunks(k_ref, vt_ref, tk):
        for h in range(Q_PER_KV):
            s_t = _scores_t(k_c, q_ref, h)
            m_old = m_ref[h]
            m_new = jnp.maximum(m_old, jnp.max(s_t, axis=0, keepdims=True))
            alpha = jnp.exp2(m_old - m_new)
            p_t = jnp.exp2(s_t - m_new)
            l_ref[h] = alpha * l_ref[h] + jnp.sum(p_t, axis=0, keepdims=True)
            acc_ref[h] = alpha * acc_ref[h] + jnp.dot(vt_c, p_t.astype(BF16), preferred_element_type=F32)
            m_ref[h] = m_new
    _attn_finish(o_ref, l_ref, acc_ref)


def _attention(q, k, vt, *, tq, tk, bounded):
    s = q.shape[0]
    group_w = Q_PER_KV * HEAD_DIM
    est = (2 * tq * group_w * 2 * 2 + 2 * s * 2 * HEAD_DIM * 2 + Q_PER_KV * (HEAD_DIM + 16) * tq * 4
           + 4 * tk * tq * 4)
    stat = pltpu.VMEM((Q_PER_KV, 1, tq), F32)
    acc = pltpu.VMEM((Q_PER_KV, HEAD_DIM, tq), F32)
    if bounded:
        body, name, scratch = _attn_body_bounded, "gqa_attention_bounded", [stat, acc]
    else:
        body, name, scratch = _attn_body_online, "gqa_attention_online", [stat, stat, acc]
    return pl.pallas_call(
        functools.partial(body, tk=tk),
        grid=(N_KV_HEADS, s // tq),
        in_specs=[
            pl.BlockSpec((tq, group_w), lambda g, i: (i, g)),
            pl.BlockSpec((None, s, HEAD_DIM), lambda g, i: (g, 0, 0)),
            pl.BlockSpec((HEAD_DIM, s), lambda g, i: (g, 0)),
        ],
        out_specs=pl.BlockSpec((tq, group_w), lambda g, i: (i, g)),
        out_shape=jax.ShapeDtypeStruct((s, ATTN_WIDTH), BF16),
        scratch_shapes=scratch,
        compiler_params=_params(est, 2),
        name=name,
    )(q, k, vt)


def _score_bound(q_g, k_g):
    q_scale = (HEAD_DIM ** -0.5) * LOG2_E
    return HEAD_DIM * q_scale * jnp.max(jnp.abs(q_g)) * jnp.max(jnp.abs(k_g))


def _fourier1_body(f_ref, chan_ref, t_ref, y_ref, ab_ref, *, tb):
    n1 = f_ref.shape[0]
    for b in range(tb):
        for g in range(FOURIER_GROUPS):
            lo = b * FOURIER_WIDTH + g * FOURIER_GROUP_DIM
            ab = jnp.dot(f_ref[:, lo:lo + FOURIER_GROUP_DIM], chan_ref[...], preferred_element_type=F32)
            cols = slice(g * FOURIER_GROUP_DIM, (g + 1) * FOURIER_GROUP_DIM)
            ab_ref[:n1, cols] = ab[:, :FOURIER_GROUP_DIM].astype(ab_ref.dtype)
            ab_ref[n1:, cols] = ab[:, FOURIER_GROUP_DIM:].astype(ab_ref.dtype)
        y = jnp.dot(t_ref[b], ab_ref[...], preferred_element_type=F32)
        y_ref[0, b] = y[:n1].astype(y_ref.dtype)
        y_ref[1, b] = y[n1:].astype(y_ref.dtype)


def _fourier2_body(y_ref, f2_ref, o_ref, *, scale):
    two, n2, w = y_ref.shape
    y = y_ref[...].reshape(two * n2, w)
    o_ref[...] = (jnp.dot(f2_ref[...], y, preferred_element_type=F32) * scale).astype(o_ref.dtype)


def _dft_tables(seq):
    n2 = SEQ_DFT_MINOR
    n1 = seq // n2
    idx = jnp.arange(n2, dtype=jnp.int32)
    ang = (2.0 * math.pi / n2) * ((idx[:, None] * idx[None, :]) % n2).astype(F32)
    cm, sm = jnp.cos(ang), jnp.sin(ang)
    chan = jnp.concatenate([cm, sm], axis=1).astype(BF16)
    minor = jnp.concatenate([cm, sm], axis=1).astype(BF16)
    k1 = jnp.arange(n1, dtype=jnp.int32)
    a1 = (2.0 * math.pi / n1) * ((k1[:, None] * k1[None, :]) % n1).astype(F32)
    a2 = (2.0 * math.pi / seq) * (jnp.arange(n2, dtype=jnp.int32)[:, None] * k1[None, :]).astype(F32)
    c1, s1 = jnp.cos(a1)[None], jnp.sin(a1)[None]
    c2, s2 = jnp.cos(a2)[:, :, None], jnp.sin(a2)[:, :, None]
    tc = c1 * c2 - s1 * s2
    ts = s1 * c2 + c1 * s2
    top = jnp.concatenate([tc, -ts], axis=2)
    bot = jnp.concatenate([-ts, -tc], axis=2)
    stage1 = jnp.concatenate([top, bot], axis=1).astype(BF16)
    return chan, stage1, minor


def _fourier(f, chan, stage1, minor, *, tb):
    s, w = f.shape
    n2 = SEQ_DFT_MINOR
    n1 = s // n2
    f_view = f.reshape(n1, n2 * w)
    est1 = 2 * n1 * tb * w * 2 + 2 * tb * 4 * n1 * n1 * 2 + 2 * 2 * tb * n1 * w * 2 + 2 * n1 * w * 2 + 4 * n1 * w * 4
    y = pl.pallas_call(
        functools.partial(_fourier1_body, tb=tb),
        grid=(n2 // tb,),
        in_specs=[
            pl.BlockSpec((n1, tb * w), lambda j: (0, j)),
            pl.BlockSpec((FOURIER_GROUP_DIM, 2 * FOURIER_GROUP_DIM), lambda j: (0, 0)),
            pl.BlockSpec((tb, 2 * n1, 2 * n1), lambda j: (j, 0, 0)),
        ],
        out_specs=pl.BlockSpec((2, tb, n1, w), lambda j: (0, j, 0, 0)),
        out_shape=jax.ShapeDtypeStruct((2, n2, n1, w), BF16),
        scratch_shapes=[pltpu.VMEM((2 * n1, w), BF16)],
        compiler_params=_params(est1, 1),
        name="fourier_stage1",
    )(f_view, chan, stage1)

    tk1 = tb
    y_view = y.reshape(2, n2, n1 * w)
    scale = 1.0 / math.sqrt(float(s) * FOURIER_GROUP_DIM)
    est2 = 2 * 2 * n2 * tk1 * w * 2 + 2 * n2 * tk1 * w * 2 + 2 * n2 * tk1 * w * 4
    out = pl.pallas_call(
        functools.partial(_fourier2_body, scale=scale),
        grid=(n1 // tk1,),
        in_specs=[
            pl.BlockSpec((2, n2, tk1 * w), lambda j: (0, 0, j)),
            pl.BlockSpec((n2, 2 * n2), lambda j: (0, 0)),
        ],
        out_specs=pl.BlockSpec((n2, tk1 * w), lambda j: (0, j)),
        out_shape=jax.ShapeDtypeStruct((n2, n1 * w), BF16),
        compiler_params=_params(est2, 1),
        name="fourier_stage2",
    )(y_view, minor)
    return out.reshape(s, w)


def _merge_body(a_ref, f_ref, ga_ref, gf_ref, h_ref, wa_ref, wf_ref, wo_ref, post_g_ref, o_ref):
    ya = jnp.dot(a_ref[...], wa_ref[...], preferred_element_type=F32)
    yf = jnp.dot(f_ref[...], wf_ref[...], preferred_element_type=F32)
    merged = ga_ref[...].astype(F32) * ya + gf_ref[...].astype(F32) * yf
    z = jnp.dot(merged.astype(BF16), wo_ref[...], preferred_element_type=F32)
    o_ref[...] = h_ref[...] + _rms(z, post_g_ref[...])


def _merge(attn, mixed, gates, h, w_attn_o, w_fourier, w_out, post_g, *, tm):
    s, d = h.shape
    resident = pl.Buffered(1)
    est = ((ATTN_WIDTH + FOURIER_WIDTH + d) * d * 2
           + 2 * tm * (ATTN_WIDTH + FOURIER_WIDTH + 2 * d) * 2 + 2 * 2 * tm * d * 4 + 4 * tm * d * 4)
    return pl.pallas_call(
        _merge_body,
        grid=(s // tm,),
        in_specs=[
            pl.BlockSpec((tm, ATTN_WIDTH), lambda i: (i, 0)),
            pl.BlockSpec((tm, FOURIER_WIDTH), lambda i: (i, 0)),
            pl.BlockSpec((tm, d), lambda i: (i, 0)),
            pl.BlockSpec((tm, d), lambda i: (i, 1)),
            pl.BlockSpec((tm, d), lambda i: (i, 0)),
            pl.BlockSpec((ATTN_WIDTH, d), lambda i: (0, 0), pipeline_mode=resident),
            pl.BlockSpec((FOURIER_WIDTH, d), lambda i: (0, 0), pipeline_mode=resident),
            pl.BlockSpec((d, d), lambda i: (0, 0), pipeline_mode=resident),
            pl.BlockSpec((1, d), lambda i: (0, 0)),
        ],
        out_specs=pl.BlockSpec((tm, d), lambda i: (i, 0)),
        out_shape=jax.ShapeDtypeStruct((s, d), F32),
        compiler_params=_params(est, 1),
        name="gated_merge",
    )(attn, mixed, gates, gates, h, w_attn_o, w_fourier, w_out, post_g)


def _tiles(seq):
    return dict(
        ffn_tm=min(512, seq), ffn_tf=512,
        proj_tm=min(1024, seq), proj_tn=1024,
        attn_tq=min(256, seq), attn_tk=min(8192, seq),
        four_tb=8,
        merge_tm=min(256, seq),
    )


def kernel(x, ffn1_pre_g, ffn1_w_gate, ffn1_w_up, ffn1_w_down, ffn1_post_g, mix_pre_g, w_in, b_gate, q_norm_g, k_norm_g, w_attn_o, w_fourier, w_out, mix_post_g, ffn2_pre_g, ffn2_w_gate, ffn2_w_up, ffn2_w_down, ffn2_post_g):
    batch, seq, d = x.shape
    assert batch == 1 and seq % SEQ_DFT_MINOR == 0
    depth = w_in.shape[0]
    t = _tiles(seq)
    cos_tab, sin_tab = _rope_tables(seq)
    chan, stage1, minor = _dft_tables(seq)
    row = lambda v: v.reshape(1, -1).astype(F32)

    h = x.reshape(seq, d)
    for l in range(depth):
        h = _ffn(h, row(ffn1_pre_g[l]), ffn1_w_gate[l].astype(BF16), ffn1_w_up[l].astype(BF16),
                 ffn1_w_down[l].astype(BF16), row(ffn1_post_g[l]), tm=t["ffn_tm"], tf=t["ffn_tf"])

        n_in = w_in.shape[2]
        bias_full = jnp.concatenate([jnp.zeros((1, n_in - 2 * d), F32), row(b_gate[l])], axis=1)
        n_qk = ATTN_WIDTH + KV_WIDTH
        w_proj = w_in[l].astype(BF16)
        w_proj = w_proj.at[:, :n_qk].set(_rope_head_perm(w_proj[:, :n_qk], N_Q_HEADS + N_KV_HEADS))
        q_scale = (HEAD_DIM ** -0.5) * LOG2_E
        q_g2 = jnp.tile(_rope_head_perm(q_norm_g[l].astype(F32), 1) * q_scale, 2).reshape(1, -1)
        k_g2 = jnp.tile(_rope_head_perm(k_norm_g[l].astype(F32), 1), 2).reshape(1, -1)
        q, k, vt, f, gates = _in_proj(h, row(mix_pre_g[l]), w_proj, bias_full, q_g2, k_g2, cos_tab, sin_tab,
                                      tm=t["proj_tm"], tn=t["proj_tn"])

        attn = lax.cond(
            _score_bound(q_norm_g[l], k_norm_g[l]) <= SCORE_BOUND_LIMIT,
            functools.partial(_attention, tq=t["attn_tq"], tk=t["attn_tk"], bounded=True),
            functools.partial(_attention, tq=t["attn_tq"], tk=t["attn_tk"], bounded=False),
            q, k, vt)

        mixed = _fourier(f, chan, stage1, minor, tb=t["four_tb"])

        h = _merge(attn, mixed, gates, h, w_attn_o[l].astype(BF16), w_fourier[l].astype(BF16),
                   w_out[l].astype(BF16), row(mix_post_g[l]), tm=t["merge_tm"])

        h = _ffn(h, row(ffn2_pre_g[l]), ffn2_w_gate[l].astype(BF16), ffn2_w_up[l].astype(BF16),
                 ffn2_w_down[l].astype(BF16), row(ffn2_post_g[l]), tm=t["ffn_tm"], tf=t["ffn_tf"])
    return h.reshape(batch, seq, d)
```

```python
import functools
import math

import jax
import jax.numpy as jnp
from jax import lax
from jax.experimental import pallas as pl
from jax.experimental.pallas import tpu as pltpu

HEAD_DIM = 128
N_KV_HEADS = 4
Q_PER_KV = 4
N_Q_HEADS = N_KV_HEADS * Q_PER_KV
ATTN_WIDTH = N_Q_HEADS * HEAD_DIM
KV_WIDTH = N_KV_HEADS * HEAD_DIM
FOURIER_GROUPS = 8
FOURIER_GROUP_DIM = 128
FOURIER_WIDTH = FOURIER_GROUPS * FOURIER_GROUP_DIM
ROPE_HALF = HEAD_DIM // 4
ROPE_THETA = 10000.0
GRID_W = 64
EPS = 1e-6
MACARON_WEIGHT = 0.5
LOG2_E = 1.4426950408889634

V7X_VMEM_BYTES = 64 * 1024 * 1024
V7X_LANES = 128
SEQ_DFT_MINOR = 128

BF16 = jnp.bfloat16
F32 = jnp.float32


def _vmem_limit(estimate_bytes):
    return int(min(estimate_bytes * 5 // 4 + (4 << 20), V7X_VMEM_BYTES - (6 << 20)))


def _params(estimate_bytes, n_axes):
    return pltpu.CompilerParams(
        dimension_semantics=("arbitrary",) * n_axes,
        vmem_limit_bytes=_vmem_limit(estimate_bytes),
    )


def _rms(x, g):
    ms = jnp.mean(x * x, axis=-1, keepdims=True)
    return x * lax.rsqrt(ms + EPS) * g


def _ffn_body(x_ref, pre_g_ref, wg_ref, wu_ref, wd_ref, post_g_ref, o_ref, xn_ref):
    j = pl.program_id(1)

    @pl.when(j == 0)
    def _():
        xn_ref[...] = _rms(x_ref[...], pre_g_ref[...]).astype(xn_ref.dtype)
        o_ref[...] = jnp.zeros(o_ref.shape, o_ref.dtype)

    xn = xn_ref[...]
    half = wg_ref.shape[1] // 2
    hidden = []
    for lo in (0, half):
        gate = jnp.dot(xn, wg_ref[:, lo:lo + half], preferred_element_type=F32)
        up = jnp.dot(xn, wu_ref[:, lo:lo + half], preferred_element_type=F32)
        hidden.append((gate * jax.nn.sigmoid(gate) * up).astype(BF16))
    o_ref[...] += jnp.dot(jnp.concatenate(hidden, axis=1), wd_ref[...], preferred_element_type=F32)

    @pl.when(j == pl.num_programs(1) - 1)
    def _():
        o_ref[...] = x_ref[...] + _rms(o_ref[...], post_g_ref[...])


def _ffn(x, pre_g, w_gate, w_up, w_down, post_g, *, tm, tf):
    s, d = x.shape
    post_g = MACARON_WEIGHT * post_g
    d_ff = w_gate.shape[1]
    est = 2 * tm * d * 4 * 2 + tm * d * 2 + 2 * 3 * d * tf * 2 + 3 * tm * tf * 4 + tm * d * 4
    return pl.pallas_call(
        _ffn_body,
        grid=(s // tm, d_ff // tf),
        in_specs=[
            pl.BlockSpec((tm, d), lambda i, j: (i, 0)),
            pl.BlockSpec((1, d), lambda i, j: (0, 0)),
            pl.BlockSpec((d, tf), lambda i, j: (0, j)),
            pl.BlockSpec((d, tf), lambda i, j: (0, j)),
            pl.BlockSpec((tf, d), lambda i, j: (j, 0)),
            pl.BlockSpec((1, d), lambda i, j: (0, 0)),
        ],
        out_specs=pl.BlockSpec((tm, d), lambda i, j: (i, 0)),
        out_shape=jax.ShapeDtypeStruct((s, d), F32),
        scratch_shapes=[pltpu.VMEM((tm, d), BF16)],
        compiler_params=_params(est, 2),
        name="macaron_ffn",
    )(x, pre_g, w_gate, w_up, w_down, post_g)


PROJ_PIECE = 2 * HEAD_DIM


def _norm_rope_pair(y, head_mean, g2, cos, sin):
    ms = jnp.dot((y * y).astype(BF16), head_mean, preferred_element_type=F32)
    yn = y * lax.rsqrt(ms + EPS) * g2
    outs = []
    for h in range(2):
        yh = yn[:, h * HEAD_DIM:(h + 1) * HEAD_DIM]
        outs.append(yh * cos + pltpu.roll(yh, HEAD_DIM // 2, 1) * sin)
    return outs


def _inproj_body(x_ref, pre_g_ref, w_ref, bias_ref, qg_ref, kg_ref, cos_ref, sin_ref, mean_ref,
                 q_ref, k_ref, vt_ref, f_ref, gate_ref, xn_ref, *, tn):
    j = pl.program_id(1)
    n_q_tiles = ATTN_WIDTH // tn
    kv_tile = n_q_tiles
    f_tile = kv_tile + 1

    @pl.when(j == 0)
    def _():
        xn_ref[...] = _rms(x_ref[...], pre_g_ref[...]).astype(xn_ref.dtype)

    def piece(lo, width=PROJ_PIECE):
        return jnp.dot(xn_ref[...], w_ref[:, lo:lo + width], preferred_element_type=F32)

    def normed_heads(width, g_ref):
        for lo2 in range(0, width, 2 * PROJ_PIECE):
            ys = [piece(lo) for lo in range(lo2, lo2 + 2 * PROJ_PIECE, PROJ_PIECE)]
            for p, y in enumerate(ys):
                outs = _norm_rope_pair(y, mean_ref[...], g_ref[...], cos_ref[...], sin_ref[...])
                for h, out in enumerate(outs):
                    yield (lo2 + p * PROJ_PIECE) // HEAD_DIM + h, out

    @pl.when(j < n_q_tiles)
    def _():
        for head, out in normed_heads(tn, qg_ref):
            q_ref[:, head * HEAD_DIM:(head + 1) * HEAD_DIM] = out.astype(q_ref.dtype)

    @pl.when(j == kv_tile)
    def _():
        for head, out in normed_heads(KV_WIDTH, kg_ref):
            k_ref[head] = out.astype(k_ref.dtype)
        for lo in range(KV_WIDTH, 2 * KV_WIDTH, PROJ_PIECE):
            vt_ref[lo - KV_WIDTH:lo - KV_WIDTH + PROJ_PIECE, :] = piece(lo).astype(vt_ref.dtype).T

    @pl.when(j == f_tile)
    def _():
        for lo in range(0, tn, PROJ_PIECE):
            f_ref[:, lo:lo + PROJ_PIECE] = piece(lo).astype(f_ref.dtype)

    @pl.when(j > f_tile)
    def _():
        for lo in range(0, tn, PROJ_PIECE):
            z = piece(lo) + bias_ref[:, lo:lo + PROJ_PIECE]
            gate_ref[:, lo:lo + PROJ_PIECE] = (0.5 * jnp.tanh(0.5 * z) + 0.5).astype(gate_ref.dtype)


def _in_proj(h, pre_g, w_in, bias_full, q_g2, k_g2, cos_tab, sin_tab, *, tm, tn):
    s, d = h.shape
    n = w_in.shape[1]
    assert tn == 2 * KV_WIDTH == FOURIER_WIDTH and ATTN_WIDTH % tn == 0
    n_q_tiles = ATTN_WIDTH // tn
    gate_tile0 = n_q_tiles + 2
    n_gate_tiles = n // tn - gate_tile0
    lane_head = jnp.arange(PROJ_PIECE, dtype=jnp.int32) // HEAD_DIM
    head_mean = jnp.where(lane_head[:, None] == lane_head[None, :], 1.0 / HEAD_DIM, 0.0).astype(BF16)
    est = (2 * tm * d * 4 + tm * d * 2 + 2 * d * tn * 2 + 2 * 2 * tm * tn * 2 * 2 + 2 * 2 * tm * KV_WIDTH * 2
           + 4 * tm * PROJ_PIECE * 4 + 4 * tm * HEAD_DIM * 4)
    return pl.pallas_call(
        functools.partial(_inproj_body, tn=tn),
        grid=(s // tm, n // tn),
        in_specs=[
            pl.BlockSpec((tm, d), lambda i, j: (i, 0)),
            pl.BlockSpec((1, d), lambda i, j: (0, 0)),
            pl.BlockSpec((d, tn), lambda i, j: (0, j)),
            pl.BlockSpec((1, tn), lambda i, j: (0, j)),
            pl.BlockSpec((1, PROJ_PIECE), lambda i, j: (0, 0)),
            pl.BlockSpec((1, PROJ_PIECE), lambda i, j: (0, 0)),
            pl.BlockSpec((tm, HEAD_DIM), lambda i, j: (i, 0)),
            pl.BlockSpec((tm, HEAD_DIM), lambda i, j: (i, 0)),
            pl.BlockSpec((PROJ_PIECE, PROJ_PIECE), lambda i, j: (0, 0)),
        ],
        out_specs=[
            pl.BlockSpec((tm, tn), lambda i, j: (i, jnp.minimum(j, n_q_tiles - 1))),
            pl.BlockSpec((N_KV_HEADS, tm, HEAD_DIM), lambda i, j: (0, i, 0)),
            pl.BlockSpec((KV_WIDTH, tm), lambda i, j: (0, i)),
            pl.BlockSpec((tm, FOURIER_WIDTH), lambda i, j: (i, 0)),
            pl.BlockSpec((tm, tn), lambda i, j: (i, jnp.clip(j - gate_tile0, 0, n_gate_tiles - 1))),
        ],
        out_shape=[
            jax.ShapeDtypeStruct((s, ATTN_WIDTH), BF16),
            jax.ShapeDtypeStruct((N_KV_HEADS, s, HEAD_DIM), BF16),
            jax.ShapeDtypeStruct((KV_WIDTH, s), BF16),
            jax.ShapeDtypeStruct((s, FOURIER_WIDTH), BF16),
            jax.ShapeDtypeStruct((s, n_gate_tiles * tn), BF16),
        ],
        scratch_shapes=[pltpu.VMEM((tm, d), BF16)],
        compiler_params=_params(est, 2),
        name="in_proj",
    )(h, pre_g, w_in, bias_full, q_g2, k_g2, cos_tab, sin_tab, head_mean)


def _rope_head_perm(v, n_heads):
    lead = v.shape[:-1]
    v = v.reshape(lead + (n_heads, 2, 2, ROPE_HALF))
    return jnp.swapaxes(v, -3, -2).reshape(lead + (n_heads * HEAD_DIM,))


def _rope_tables(seq):
    n_rows = seq // GRID_W
    n_freq = ROPE_HALF
    inv_freq = ROPE_THETA ** (-jnp.arange(n_freq, dtype=F32) / n_freq)
    ang_r = jnp.arange(n_rows, dtype=F32)[:, None] * inv_freq
    ang_c = jnp.arange(GRID_W, dtype=F32)[:, None] * inv_freq
    by_row = lambda v: jnp.repeat(v, GRID_W, axis=0)
    by_col = lambda v: jnp.tile(v, (n_rows, 1))
    cos_r, sin_r = by_row(jnp.cos(ang_r)), by_row(jnp.sin(ang_r))
    cos_c, sin_c = by_col(jnp.cos(ang_c)), by_col(jnp.sin(ang_c))
    cos_tab = jnp.concatenate([cos_r, cos_c, cos_r, cos_c], axis=-1)
    sin_tab = jnp.concatenate([-sin_r, -sin_c, sin_r, sin_c], axis=-1)
    return cos_tab, sin_tab


SCORE_BOUND_LIMIT = 100.0


def _kv_chunks(k_ref, vt_ref, tk):
    for c in range(k_ref.shape[0] // tk):
        yield k_ref[c * tk:(c + 1) * tk, :], vt_ref[:, c * tk:(c + 1) * tk]


def _scores_t(k_c, q_ref, h):
    q_h = q_ref[:, h * HEAD_DIM:(h + 1) * HEAD_DIM]
    return lax.dot_general(k_c, q_h, (((1,), (1,)), ((), ())), preferred_element_type=F32)


def _attn_finish(o_ref, l_ref, acc_ref):
    for h in range(Q_PER_KV):
        o_t = acc_ref[h] / l_ref[h]
        o_ref[:, h * HEAD_DIM:(h + 1) * HEAD_DIM] = o_t.T.astype(o_ref.dtype)


def _attn_body_bounded(q_ref, k_ref, vt_ref, o_ref, l_ref, acc_ref, *, tk):
    l_ref[...] = jnp.zeros(l_ref.shape, F32)
    acc_ref[...] = jnp.zeros(acc_ref.shape, F32)
    for k_c, vt_c in _kv_chunks(k_ref, vt_ref, tk):
        s_next = _scores_t(k_c, q_ref, 0)
        for h in range(Q_PER_KV):
            s_t = s_next
            if h + 1 < Q_PER_KV:
                s_next = _scores_t(k_c, q_ref, h + 1)
            p_t = jnp.exp2(s_t)
            l_ref[h] += jnp.sum(p_t, axis=0, keepdims=True)
            acc_ref[h] += jnp.dot(vt_c, p_t.astype(BF16), preferred_element_type=F32)
    _attn_finish(o_ref, l_ref, acc_ref)


def _attn_body_online(q_ref, k_ref, vt_ref, o_ref, m_ref, l_ref, acc_ref, *, tk):
    m_ref[...] = jnp.full(m_ref.shape, -jnp.inf, F32)
    l_ref[...] = jnp.zeros(l_ref.shape, F32)
    acc_ref[...] = jnp.zeros(acc_ref.shape, F32)
    for k_c, vt_c in _kv_chunks(k_ref, vt_ref, tk):
        for h in range(Q_PER_KV):
            s_t = _scores_t(k_c, q_ref, h)
            m_old = m_ref[h]
            m_new = jnp.maximum(m_old, jnp.max(s_t, axis=0, keepdims=True))
            alpha = jnp.exp2(m_old - m_new)
            p_t = jnp.exp2(s_t - m_new)
            l_ref[h] = alpha * l_ref[h] + jnp.sum(p_t, axis=0, keepdims=True)
            acc_ref[h] = alpha * acc_ref[h] + jnp.dot(vt_c, p_t.astype(BF16), preferred_element_type=F32)
            m_ref[h] = m_new
    _attn_finish(o_ref, l_ref, acc_ref)


def _attention(q, k, vt, *, tq, tk, bounded):
    s = q.shape[0]
    group_w = Q_PER_KV * HEAD_DIM
    est = (2 * tq * group_w * 2 * 2 + 2 * s * 2 * HEAD_DIM * 2 + Q_PER_KV * (HEAD_DIM + 16) * tq * 4
           + 4 * tk * tq * 4)
    stat = pltpu.VMEM((Q_PER_KV, 1, tq), F32)
    acc = pltpu.VMEM((Q_PER_KV, HEAD_DIM, tq), F32)
    if bounded:
        body, name, scratch = _attn_body_bounded, "gqa_attention_bounded", [stat, acc]
    else:
        body, name, scratch = _attn_body_online, "gqa_attention_online", [stat, stat, acc]
    return pl.pallas_call(
        functools.partial(body, tk=tk),
        grid=(N_KV_HEADS, s // tq),
        in_specs=[
            pl.BlockSpec((tq, group_w), lambda g, i: (i, g)),
            pl.BlockSpec((None, s, HEAD_DIM), lambda g, i: (g, 0, 0)),
            pl.BlockSpec((HEAD_DIM, s), lambda g, i: (g, 0)),
        ],
        out_specs=pl.BlockSpec((tq, group_w), lambda g, i: (i, g)),
        out_shape=jax.ShapeDtypeStruct((s, ATTN_WIDTH), BF16),
        scratch_shapes=scratch,
        compiler_params=_params(est, 2),
        name=name,
    )(q, k, vt)


def _score_bound(q_g, k_g):
    q_scale = (HEAD_DIM ** -0.5) * LOG2_E
    return HEAD_DIM * q_scale * jnp.max(jnp.abs(q_g)) * jnp.max(jnp.abs(k_g))


def _fourier1_body(f_ref, chan_ref, t_ref, y_ref, ab_ref, *, tb):
    n1 = f_ref.shape[0]
    for b in range(tb):
        for g in range(FOURIER_GROUPS):
            lo = b * FOURIER_WIDTH + g * FOURIER_GROUP_DIM
            ab = jnp.dot(f_ref[:, lo:lo + FOURIER_GROUP_DIM], chan_ref[...], preferred_element_type=F32)
            cols = slice(g * FOURIER_GROUP_DIM, (g + 1) * FOURIER_GROUP_DIM)
            ab_ref[:n1, cols] = ab[:, :FOURIER_GROUP_DIM].astype(ab_ref.dtype)
            ab_ref[n1:, cols] = ab[:, FOURIER_GROUP_DIM:].astype(ab_ref.dtype)
        y = jnp.dot(t_ref[b], ab_ref[...], preferred_element_type=F32)
        y_ref[0, b] = y[:n1].astype(y_ref.dtype)
        y_ref[1, b] = y[n1:].astype(y_ref.dtype)


def _fourier2_body(y_ref, f2_ref, o_ref, *, scale):
    two, n2, w = y_ref.shape
    y = y_ref[...].reshape(two * n2, w)
    o_ref[...] = (jnp.dot(f2_ref[...], y, preferred_element_type=F32) * scale).astype(o_ref.dtype)


def _dft_tables(seq):
    n2 = SEQ_DFT_MINOR
    n1 = seq // n2
    idx = jnp.arange(n2, dtype=jnp.int32)
    ang = (2.0 * math.pi / n2) * ((idx[:, None] * idx[None, :]) % n2).astype(F32)
    cm, sm = jnp.cos(ang), jnp.sin(ang)
    chan = jnp.concatenate([cm, sm], axis=1).astype(BF16)
    minor = jnp.concatenate([cm, sm], axis=1).astype(BF16)
    k1 = jnp.arange(n1, dtype=jnp.int32)
    a1 = (2.0 * math.pi / n1) * ((k1[:, None] * k1[None, :]) % n1).astype(F32)
    a2 = (2.0 * math.pi / seq) * (jnp.arange(n2, dtype=jnp.int32)[:, None] * k1[None, :]).astype(F32)
    c1, s1 = jnp.cos(a1)[None], jnp.sin(a1)[None]
    c2, s2 = jnp.cos(a2)[:, :, None], jnp.sin(a2)[:, :, None]
    tc = c1 * c2 - s1 * s2
    ts = s1 * c2 + c1 * s2
    top = jnp.concatenate([tc, -ts], axis=2)
    bot = jnp.concatenate([-ts, -tc], axis=2)
    stage1 = jnp.concatenate([top, bot], axis=1).astype(BF16)
    return chan, stage1, minor


def _fourier(f, chan, stage1, minor, *, tb):
    s, w = f.shape
    n2 = SEQ_DFT_MINOR
    n1 = s // n2
    f_view = f.reshape(n1, n2 * w)
    est1 = 2 * n1 * tb * w * 2 + 2 * tb * 4 * n1 * n1 * 2 + 2 * 2 * tb * n1 * w * 2 + 2 * n1 * w * 2 + 4 * n1 * w * 4
    y = pl.pallas_call(
        functools.partial(_fourier1_body, tb=tb),
        grid=(n2 // tb,),
        in_specs=[
            pl.BlockSpec((n1, tb * w), lambda j: (0, j)),
            pl.BlockSpec((FOURIER_GROUP_DIM, 2 * FOURIER_GROUP_DIM), lambda j: (0, 0)),
            pl.BlockSpec((tb, 2 * n1, 2 * n1), lambda j: (j, 0, 0)),
        ],
        out_specs=pl.BlockSpec((2, tb, n1, w), lambda j: (0, j, 0, 0)),
        out_shape=jax.ShapeDtypeStruct((2, n2, n1, w), BF16),
        scratch_shapes=[pltpu.VMEM((2 * n1, w), BF16)],
        compiler_params=_params(est1, 1),
        name="fourier_stage1",
    )(f_view, chan, stage1)

    tk1 = tb
    y_view = y.reshape(2, n2, n1 * w)
    scale = 1.0 / math.sqrt(float(s) * FOURIER_GROUP_DIM)
    est2 = 2 * 2 * n2 * tk1 * w * 2 + 2 * n2 * tk1 * w * 2 + 2 * n2 * tk1 * w * 4
    out = pl.pallas_call(
        functools.partial(_fourier2_body, scale=scale),
        grid=(n1 // tk1,),
        in_specs=[
            pl.BlockSpec((2, n2, tk1 * w), lambda j: (0, 0, j)),
            pl.BlockSpec((n2, 2 * n2), lambda j: (0, 0)),
        ],
        out_specs=pl.BlockSpec((n2, tk1 * w), lambda j: (0, j)),
        out_shape=jax.ShapeDtypeStruct((n2, n1 * w), BF16),
        compiler_params=_params(est2, 1),
        name="fourier_stage2",
    )(y_view, minor)
    return out.reshape(s, w)


def _merge_body(a_ref, f_ref, ga_ref, gf_ref, h_ref, wa_ref, wf_ref, wo_ref, post_g_ref, o_ref):
    ya = jnp.dot(a_ref[...], wa_ref[...], preferred_element_type=F32)
    yf = jnp.dot(f_ref[...], wf_ref[...], preferred_element_type=F32)
    merged = ga_ref[...].astype(F32) * ya + gf_ref[...].astype(F32) * yf
    z = jnp.dot(merged.astype(BF16), wo_ref[...], preferred_element_type=F32)
    o_ref[...] = h_ref[...] + _rms(z, post_g_ref[...])


def _merge(attn, mixed, gates, h, w_attn_o, w_fourier, w_out, post_g, *, tm):
    s, d = h.shape
    resident = pl.Buffered(1)
    est = ((ATTN_WIDTH + FOURIER_WIDTH + d) * d * 2
           + 2 * tm * (ATTN_WIDTH + FOURIER_WIDTH + 2 * d) * 2 + 2 * 2 * tm * d * 4 + 4 * tm * d * 4)
    return pl.pallas_call(
        _merge_body,
        grid=(s // tm,),
        in_specs=[
            pl.BlockSpec((tm, ATTN_WIDTH), lambda i: (i, 0)),
            pl.BlockSpec((tm, FOURIER_WIDTH), lambda i: (i, 0)),
            pl.BlockSpec((tm, d), lambda i: (i, 0)),
            pl.BlockSpec((tm, d), lambda i: (i, 1)),
            pl.BlockSpec((tm, d), lambda i: (i, 0)),
            pl.BlockSpec((ATTN_WIDTH, d), lambda i: (0, 0), pipeline_mode=resident),
            pl.BlockSpec((FOURIER_WIDTH, d), lambda i: (0, 0), pipeline_mode=resident),
            pl.BlockSpec((d, d), lambda i: (0, 0), pipeline_mode=resident),
            pl.BlockSpec((1, d), lambda i: (0, 0)),
        ],
        out_specs=pl.BlockSpec((tm, d), lambda i: (i, 0)),
        out_shape=jax.ShapeDtypeStruct((s, d), F32),
        compiler_params=_params(est, 1),
        name="gated_merge",
    )(attn, mixed, gates, gates, h, w_attn_o, w_fourier, w_out, post_g)


def _tiles(seq):
    return dict(
        ffn_tm=min(512, seq), ffn_tf=512,
        proj_tm=min(1024, seq), proj_tn=1024,
        attn_tq=min(512, seq), attn_tk=min(8192, seq),
        four_tb=8,
        merge_tm=min(256, seq),
    )


def kernel(x, ffn1_pre_g, ffn1_w_gate, ffn1_w_up, ffn1_w_down, ffn1_post_g, mix_pre_g, w_in, b_gate, q_norm_g, k_norm_g, w_attn_o, w_fourier, w_out, mix_post_g, ffn2_pre_g, ffn2_w_gate, ffn2_w_up, ffn2_w_down, ffn2_post_g):
    batch, seq, d = x.shape
    assert batch == 1 and seq % SEQ_DFT_MINOR == 0
    depth = w_in.shape[0]
    t = _tiles(seq)
    cos_tab, sin_tab = _rope_tables(seq)
    chan, stage1, minor = _dft_tables(seq)
    row = lambda v: v.reshape(1, -1).astype(F32)

    h = x.reshape(seq, d)
    for l in range(depth):
        h = _ffn(h, row(ffn1_pre_g[l]), ffn1_w_gate[l].astype(BF16), ffn1_w_up[l].astype(BF16),
                 ffn1_w_down[l].astype(BF16), row(ffn1_post_g[l]), tm=t["ffn_tm"], tf=t["ffn_tf"])

        n_in = w_in.shape[2]
        bias_full = jnp.concatenate([jnp.zeros((1, n_in - 2 * d), F32), row(b_gate[l])], axis=1)
        n_qk = ATTN_WIDTH + KV_WIDTH
        w_proj = w_in[l].astype(BF16)
        w_proj = w_proj.at[:, :n_qk].set(_rope_head_perm(w_proj[:, :n_qk], N_Q_HEADS + N_KV_HEADS))
        q_scale = (HEAD_DIM ** -0.5) * LOG2_E
        q_g2 = jnp.tile(_rope_head_perm(q_norm_g[l].astype(F32), 1) * q_scale, 2).reshape(1, -1)
        k_g2 = jnp.tile(_rope_head_perm(k_norm_g[l].astype(F32), 1), 2).reshape(1, -1)
        q, k, vt, f, gates = _in_proj(h, row(mix_pre_g[l]), w_proj, bias_full, q_g2, k_g2, cos_tab, sin_tab,
                                      tm=t["proj_tm"], tn=t["proj_tn"])

        attn = lax.cond(
            _score_bound(q_norm_g[l], k_norm_g[l]) <= SCORE_BOUND_LIMIT,
            functools.partial(_attention, tq=t["attn_tq"], tk=t["attn_tk"], bounded=True),
            functools.partial(_attention, tq=t["attn_tq"], tk=t["attn_tk"], bounded=False),
            q, k, vt)

        mixed = _fourier(f, chan, stage1, minor, tb=t["four_tb"])

        h = _merge(attn, mixed, gates, h, w_attn_o[l].astype(BF16), w_fourier[l].astype(BF16),
                   w_out[l].astype(BF16), row(mix_post_g[l]), tm=t["merge_tm"])

        h = _ffn(h, row(ffn2_pre_g[l]), ffn2_w_gate[l].astype(BF16), ffn2_w_up[l].astype(BF16),
                 ffn2_w_down[l].astype(BF16), row(ffn2_post_g[l]), tm=t["ffn_tm"], tf=t["ffn_tf"])
    return h.reshape(batch, seq, d)
```

```python
import functools
import math

import jax
import jax.numpy as jnp
from jax import lax
from jax.experimental import pallas as pl
from jax.experimental.pallas import tpu as pltpu

HEAD_DIM = 128
N_KV_HEADS = 4
Q_PER_KV = 4
N_Q_HEADS = N_KV_HEADS * Q_PER_KV
ATTN_WIDTH = N_Q_HEADS * HEAD_DIM
KV_WIDTH = N_KV_HEADS * HEAD_DIM
FOURIER_GROUPS = 8
FOURIER_GROUP_DIM = 128
FOURIER_WIDTH = FOURIER_GROUPS * FOURIER_GROUP_DIM
ROPE_HALF = HEAD_DIM // 4
ROPE_THETA = 10000.0
GRID_W = 64
EPS = 1e-6
MACARON_WEIGHT = 0.5
LOG2_E = 1.4426950408889634

V7X_VMEM_BYTES = 64 * 1024 * 1024
V7X_LANES = 128
SEQ_DFT_MINOR = 128

BF16 = jnp.bfloat16
F32 = jnp.float32


def _vmem_limit(estimate_bytes):
    return int(min(estimate_bytes * 5 // 4 + (4 << 20), V7X_VMEM_BYTES - (6 << 20)))


def _params(estimate_bytes, n_axes):
    return pltpu.CompilerParams(
        dimension_semantics=("arbitrary",) * n_axes,
        vmem_limit_bytes=_vmem_limit(estimate_bytes),
    )


def _rms(x, g):
    ms = jnp.mean(x * x, axis=-1, keepdims=True)
    return x * lax.rsqrt(ms + EPS) * g


def _ffn_body(x_ref, pre_g_ref, wg_ref, wu_ref, wd_ref, post_g_ref, o_ref, xn_ref, inv_ref):
    j = pl.program_id(1)

    @pl.when(j == 0)
    def _():
        xn_ref[...] = _rms(x_ref[...], pre_g_ref[...]).astype(xn_ref.dtype)
        o_ref[...] = jnp.zeros(o_ref.shape, o_ref.dtype)

    xn = xn_ref[...]
    half = wg_ref.shape[1] // 2
    hidden = []
    for lo in (0, half):
        gate = jnp.dot(xn, wg_ref[:, lo:lo + half], preferred_element_type=F32)
        up = jnp.dot(xn, wu_ref[:, lo:lo + half], preferred_element_type=F32)
        hidden.append((gate * jax.nn.sigmoid(gate) * up).astype(BF16))
    o_ref[...] += jnp.dot(jnp.concatenate(hidden, axis=1), wd_ref[...], preferred_element_type=F32)

    last = pl.num_programs(1) - 1

    @pl.when(j == last)
    def _():
        y = o_ref[...]
        inv_ref[...] = lax.rsqrt(jnp.mean(y * y, axis=-1, keepdims=True) + EPS)

    @pl.when(j == last)
    def _():
        o_ref[...] = x_ref[...] + o_ref[...] * inv_ref[...] * post_g_ref[...]


def _ffn(x, pre_g, w_gate, w_up, w_down, post_g, *, tm, tf):
    s, d = x.shape
    post_g = MACARON_WEIGHT * post_g
    d_ff = w_gate.shape[1]
    est = 2 * tm * d * 4 * 2 + tm * d * 2 + 2 * 3 * d * tf * 2 + 3 * tm * tf * 4 + tm * d * 4
    return pl.pallas_call(
        _ffn_body,
        grid=(s // tm, d_ff // tf),
        in_specs=[
            pl.BlockSpec((tm, d), lambda i, j: (i, 0)),
            pl.BlockSpec((1, d), lambda i, j: (0, 0)),
            pl.BlockSpec((d, tf), lambda i, j: (0, j)),
            pl.BlockSpec((d, tf), lambda i, j: (0, j)),
            pl.BlockSpec((tf, d), lambda i, j: (j, 0)),
            pl.BlockSpec((1, d), lambda i, j: (0, 0)),
        ],
        out_specs=pl.BlockSpec((tm, d), lambda i, j: (i, 0)),
        out_shape=jax.ShapeDtypeStruct((s, d), F32),
        scratch_shapes=[pltpu.VMEM((tm, d), BF16), pltpu.VMEM((tm, 1), F32)],
        compiler_params=_params(est, 2),
        name="macaron_ffn",
    )(x, pre_g, w_gate, w_up, w_down, post_g)


PROJ_PIECE = 2 * HEAD_DIM


def _norm_rope_pair(y, head_mean, g2, cos, sin):
    ms = jnp.dot((y * y).astype(BF16), head_mean, preferred_element_type=F32)
    yn = y * lax.rsqrt(ms + EPS) * g2
    outs = []
    for h in range(2):
        yh = yn[:, h * HEAD_DIM:(h + 1) * HEAD_DIM]
        outs.append(yh * cos + pltpu.roll(yh, HEAD_DIM // 2, 1) * sin)
    return outs


def _inproj_body(x_ref, pre_g_ref, w_ref, bias_ref, qg_ref, kg_ref, cos_ref, sin_ref, mean_ref,
                 q_ref, k_ref, vt_ref, f_ref, gate_ref, xn_ref, *, tn):
    j = pl.program_id(1)
    n_q_tiles = ATTN_WIDTH // tn
    kv_tile = n_q_tiles
    f_tile = kv_tile + 1

    @pl.when(j == 0)
    def _():
        xn_ref[...] = _rms(x_ref[...], pre_g_ref[...]).astype(xn_ref.dtype)

    def piece(lo, width=PROJ_PIECE):
        return jnp.dot(xn_ref[...], w_ref[:, lo:lo + width], preferred_element_type=F32)

    def normed_heads(width, g_ref):
        for lo2 in range(0, width, 2 * PROJ_PIECE):
            ys = [piece(lo) for lo in range(lo2, lo2 + 2 * PROJ_PIECE, PROJ_PIECE)]
            for p, y in enumerate(ys):
                outs = _norm_rope_pair(y, mean_ref[...], g_ref[...], cos_ref[...], sin_ref[...])
                for h, out in enumerate(outs):
                    yield (lo2 + p * PROJ_PIECE) // HEAD_DIM + h, out

    @pl.when(j < n_q_tiles)
    def _():
        for head, out in normed_heads(tn, qg_ref):
            q_ref[:, head * HEAD_DIM:(head + 1) * HEAD_DIM] = out.astype(q_ref.dtype)

    @pl.when(j == kv_tile)
    def _():
        for head, out in normed_heads(KV_WIDTH, kg_ref):
            k_ref[head] = out.astype(k_ref.dtype)
        for lo in range(KV_WIDTH, 2 * KV_WIDTH, PROJ_PIECE):
            vt_ref[lo - KV_WIDTH:lo - KV_WIDTH + PROJ_PIECE, :] = piece(lo).astype(vt_ref.dtype).T

    @pl.when(j == f_tile)
    def _():
        for lo in range(0, tn, PROJ_PIECE):
            f_ref[:, lo:lo + PROJ_PIECE] = piece(lo).astype(f_ref.dtype)

    @pl.when(j > f_tile)
    def _():
        for lo in range(0, tn, PROJ_PIECE):
            z = piece(lo) + bias_ref[:, lo:lo + PROJ_PIECE]
            gate_ref[:, lo:lo + PROJ_PIECE] = (0.5 * jnp.tanh(0.5 * z) + 0.5).astype(gate_ref.dtype)


def _in_proj(h, pre_g, w_in, bias_full, q_g2, k_g2, cos_tab, sin_tab, *, tm, tn):
    s, d = h.shape
    n = w_in.shape[1]
    assert tn == 2 * KV_WIDTH == FOURIER_WIDTH and ATTN_WIDTH % tn == 0
    n_q_tiles = ATTN_WIDTH // tn
    gate_tile0 = n_q_tiles + 2
    n_gate_tiles = n // tn - gate_tile0
    lane_head = jnp.arange(PROJ_PIECE, dtype=jnp.int32) // HEAD_DIM
    head_mean = jnp.where(lane_head[:, None] == lane_head[None, :], 1.0 / HEAD_DIM, 0.0).astype(BF16)
    est = (2 * tm * d * 4 + tm * d * 2 + 2 * d * tn * 2 + 2 * 2 * tm * tn * 2 * 2 + 2 * 2 * tm * KV_WIDTH * 2
           + 4 * tm * PROJ_PIECE * 4 + 4 * tm * HEAD_DIM * 4)
    return pl.pallas_call(
        functools.partial(_inproj_body, tn=tn),
        grid=(s // tm, n // tn),
        in_specs=[
            pl.BlockSpec((tm, d), lambda i, j: (i, 0)),
            pl.BlockSpec((1, d), lambda i, j: (0, 0)),
            pl.BlockSpec((d, tn), lambda i, j: (0, j)),
            pl.BlockSpec((1, tn), lambda i, j: (0, j)),
            pl.BlockSpec((1, PROJ_PIECE), lambda i, j: (0, 0)),
            pl.BlockSpec((1, PROJ_PIECE), lambda i, j: (0, 0)),
            pl.BlockSpec((tm, HEAD_DIM), lambda i, j: (i, 0)),
            pl.BlockSpec((tm, HEAD_DIM), lambda i, j: (i, 0)),
            pl.BlockSpec((PROJ_PIECE, PROJ_PIECE), lambda i, j: (0, 0)),
        ],
        out_specs=[
            pl.BlockSpec((tm, tn), lambda i, j: (i, jnp.minimum(j, n_q_tiles - 1))),
            pl.BlockSpec((N_KV_HEADS, tm, HEAD_DIM), lambda i, j: (0, i, 0)),
            pl.BlockSpec((KV_WIDTH, tm), lambda i, j: (0, i)),
            pl.BlockSpec((tm, FOURIER_WIDTH), lambda i, j: (i, 0)),
            pl.BlockSpec((tm, tn), lambda i, j: (i, jnp.clip(j - gate_tile0, 0, n_gate_tiles - 1))),
        ],
        out_shape=[
            jax.ShapeDtypeStruct((s, ATTN_WIDTH), BF16),
            jax.ShapeDtypeStruct((N_KV_HEADS, s, HEAD_DIM), BF16),
            jax.ShapeDtypeStruct((KV_WIDTH, s), BF16),
            jax.ShapeDtypeStruct((s, FOURIER_WIDTH), BF16),
            jax.ShapeDtypeStruct((s, n_gate_tiles * tn), BF16),
        ],
        scratch_shapes=[pltpu.VMEM((tm, d), BF16)],
        compiler_params=_params(est, 2),
        name="in_proj",
    )(h, pre_g, w_in, bias_full, q_g2, k_g2, cos_tab, sin_tab, head_mean)


def _rope_head_perm(v, n_heads):
    lead = v.shape[:-1]
    v = v.reshape(lead + (n_heads, 2, 2, ROPE_HALF))
    return jnp.swapaxes(v, -3, -2).reshape(lead + (n_heads * HEAD_DIM,))


def _rope_tables(seq):
    n_rows = seq // GRID_W
    n_freq = ROPE_HALF
    inv_freq = ROPE_THETA ** (-jnp.arange(n_freq, dtype=F32) / n_freq)
    ang_r = jnp.arange(n_rows, dtype=F32)[:, None] * inv_freq
    ang_c = jnp.arange(GRID_W, dtype=F32)[:, None] * inv_freq
    by_row = lambda v: jnp.repeat(v, GRID_W, axis=0)
    by_col = lambda v: jnp.tile(v, (n_rows, 1))
    cos_r, sin_r = by_row(jnp.cos(ang_r)), by_row(jnp.sin(ang_r))
    cos_c, sin_c = by_col(jnp.cos(ang_c)), by_col(jnp.sin(ang_c))
    cos_tab = jnp.concatenate([cos_r, cos_c, cos_r, cos_c], axis=-1)
    sin_tab = jnp.concatenate([-sin_r, -sin_c, sin_r, sin_c], axis=-1)
    return cos_tab, sin_tab


SCORE_BOUND_LIMIT = 100.0


def _kv_chunks(k_ref, vt_ref, tk):
    for c in range(k_ref.shape[0] // tk):
        yield k_ref[c * tk:(c + 1) * tk, :], vt_ref[:, c * tk:(c + 1) * tk]


def _scores_t(k_c, q_ref, h):
    q_h = q_ref[:, h * HEAD_DIM:(h + 1) * HEAD_DIM]
    return lax.dot_general(k_c, q_h, (((1,), (1,)), ((), ())), preferred_element_type=F32)


def _attn_finish(o_ref, l_ref, acc_ref):
    for h in range(Q_PER_KV):
        o_t = acc_ref[h] / l_ref[h]
        o_ref[:, h * HEAD_DIM:(h + 1) * HEAD_DIM] = o_t.T.astype(o_ref.dtype)


def _attn_body_bounded(q_ref, k_ref, vt_ref, o_ref, l_ref, acc_ref, *, tk):
    l_ref[...] = jnp.zeros(l_ref.shape, F32)
    acc_ref[...] = jnp.zeros(acc_ref.shape, F32)
    for k_c, vt_c in _kv_chunks(k_ref, vt_ref, tk):
        s_next = _scores_t(k_c, q_ref, 0)
        for h in range(Q_PER_KV):
            s_t = s_next
            if h + 1 < Q_PER_KV:
                s_next = _scores_t(k_c, q_ref, h + 1)
            p_t = jnp.exp2(s_t)
            l_ref[h] += jnp.sum(p_t, axis=0, keepdims=True)
            acc_ref[h] += jnp.dot(vt_c, p_t.astype(BF16), preferred_element_type=F32)
    _attn_finish(o_ref, l_ref, acc_ref)


def _attn_body_online(q_ref, k_ref, vt_ref, o_ref, m_ref, l_ref, acc_ref, *, tk):
    m_ref[...] = jnp.full(m_ref.shape, -jnp.inf, F32)
    l_ref[...] = jnp.zeros(l_ref.shape, F32)
    acc_ref[...] = jnp.zeros(acc_ref.shape, F32)
    for k_c, vt_c in _kv_chunks(k_ref, vt_ref, tk):
        for h in range(Q_PER_KV):
            s_t = _scores_t(k_c, q_ref, h)
            m_old = m_ref[h]
            m_new = jnp.maximum(m_old, jnp.max(s_t, axis=0, keepdims=True))
            alpha = jnp.exp2(m_old - m_new)
            p_t = jnp.exp2(s_t - m_new)
            l_ref[h] = alpha * l_ref[h] + jnp.sum(p_t, axis=0, keepdims=True)
            acc_ref[h] = alpha * acc_ref[h] + jnp.dot(vt_c, p_t.astype(BF16), preferred_element_type=F32)
            m_ref[h] = m_new
    _attn_finish(o_ref, l_ref, acc_ref)


def _attn_body(bounded_ref, q_ref, k_ref, vt_ref, o_ref, m_ref, l_ref, acc_ref, *, tk):
    @pl.when(bounded_ref[0] != 0)
    def _():
        _attn_body_bounded(q_ref, k_ref, vt_ref, o_ref, l_ref, acc_ref, tk=tk)

    @pl.when(bounded_ref[0] == 0)
    def _():
        _attn_body_online(q_ref, k_ref, vt_ref, o_ref, m_ref, l_ref, acc_ref, tk=tk)


def _attention(bounded, q, k, vt, *, tq, tk):
    s = q.shape[0]
    group_w = Q_PER_KV * HEAD_DIM
    est = (2 * tq * group_w * 2 * 2 + 2 * s * 2 * HEAD_DIM * 2 + Q_PER_KV * (HEAD_DIM + 16) * tq * 4
           + 4 * tk * tq * 4)
    stat = pltpu.VMEM((Q_PER_KV, 1, tq), F32)
    acc = pltpu.VMEM((Q_PER_KV, HEAD_DIM, tq), F32)
    return pl.pallas_call(
        functools.partial(_attn_body, tk=tk),
        grid_spec=pltpu.PrefetchScalarGridSpec(
            num_scalar_prefetch=1,
            grid=(N_KV_HEADS, s // tq),
            in_specs=[
                pl.BlockSpec((tq, group_w), lambda g, i, flag: (i, g)),
                pl.BlockSpec((None, s, HEAD_DIM), lambda g, i, flag: (g, 0, 0)),
                pl.BlockSpec((HEAD_DIM, s), lambda g, i, flag: (g, 0)),
            ],
            out_specs=pl.BlockSpec((tq, group_w), lambda g, i, flag: (i, g)),
            scratch_shapes=[stat, stat, acc],
        ),
        out_shape=jax.ShapeDtypeStruct((s, ATTN_WIDTH), BF16),
        compiler_params=_params(est, 2),
        name="gqa_attention",
    )(bounded, q, k, vt)


def _score_bound(q_g, k_g):
    q_scale = (HEAD_DIM ** -0.5) * LOG2_E
    return HEAD_DIM * q_scale * jnp.max(jnp.abs(q_g)) * jnp.max(jnp.abs(k_g))


def _fourier1_body(f_ref, chan_ref, t_ref, y_ref, ab_ref, *, tb):
    n1 = f_ref.shape[0]
    for b in range(tb):
        for g in range(FOURIER_GROUPS):
            lo = b * FOURIER_WIDTH + g * FOURIER_GROUP_DIM
            ab = jnp.dot(f_ref[:, lo:lo + FOURIER_GROUP_DIM], chan_ref[...], preferred_element_type=F32)
            cols = slice(g * FOURIER_GROUP_DIM, (g + 1) * FOURIER_GROUP_DIM)
            ab_ref[:n1, cols] = ab[:, :FOURIER_GROUP_DIM].astype(ab_ref.dtype)
            ab_ref[n1:, cols] = ab[:, FOURIER_GROUP_DIM:].astype(ab_ref.dtype)
        y = jnp.dot(t_ref[b], ab_ref[...], preferred_element_type=F32)
        y_ref[0, b] = y[:n1].astype(y_ref.dtype)
        y_ref[1, b] = y[n1:].astype(y_ref.dtype)


def _fourier2_body(y_ref, f2_ref, o_ref, *, scale):
    two, n2, w = y_ref.shape
    y = y_ref[...].reshape(two * n2, w)
    o_ref[...] = (jnp.dot(f2_ref[...], y, preferred_element_type=F32) * scale).astype(o_ref.dtype)


def _dft_tables(seq):
    n2 = SEQ_DFT_MINOR
    n1 = seq // n2
    idx = jnp.arange(n2, dtype=jnp.int32)
    ang = (2.0 * math.pi / n2) * ((idx[:, None] * idx[None, :]) % n2).astype(F32)
    cm, sm = jnp.cos(ang), jnp.sin(ang)
    chan = jnp.concatenate([cm, sm], axis=1).astype(BF16)
    minor = jnp.concatenate([cm, sm], axis=1).astype(BF16)
    k1 = jnp.arange(n1, dtype=jnp.int32)
    a1 = (2.0 * math.pi / n1) * ((k1[:, None] * k1[None, :]) % n1).astype(F32)
    a2 = (2.0 * math.pi / seq) * (jnp.arange(n2, dtype=jnp.int32)[:, None] * k1[None, :]).astype(F32)
    c1, s1 = jnp.cos(a1)[None], jnp.sin(a1)[None]
    c2, s2 = jnp.cos(a2)[:, :, None], jnp.sin(a2)[:, :, None]
    tc = c1 * c2 - s1 * s2
    ts = s1 * c2 + c1 * s2
    top = jnp.concatenate([tc, -ts], axis=2)
    bot = jnp.concatenate([-ts, -tc], axis=2)
    stage1 = jnp.concatenate([top, bot], axis=1).astype(BF16)
    return chan, stage1, minor


def _fourier(f, chan, stage1, minor, *, tb):
    s, w = f.shape
    n2 = SEQ_DFT_MINOR
    n1 = s // n2
    f_view = f.reshape(n1, n2 * w)
    est1 = 2 * n1 * tb * w * 2 + 2 * tb * 4 * n1 * n1 * 2 + 2 * 2 * tb * n1 * w * 2 + 2 * n1 * w * 2 + 4 * n1 * w * 4
    y = pl.pallas_call(
        functools.partial(_fourier1_body, tb=tb),
        grid=(n2 // tb,),
        in_specs=[
            pl.BlockSpec((n1, tb * w), lambda j: (0, j)),
            pl.BlockSpec((FOURIER_GROUP_DIM, 2 * FOURIER_GROUP_DIM), lambda j: (0, 0)),
            pl.BlockSpec((tb, 2 * n1, 2 * n1), lambda j: (j, 0, 0)),
        ],
        out_specs=pl.BlockSpec((2, tb, n1, w), lambda j: (0, j, 0, 0)),
        out_shape=jax.ShapeDtypeStruct((2, n2, n1, w), BF16),
        scratch_shapes=[pltpu.VMEM((2 * n1, w), BF16)],
        compiler_params=_params(est1, 1),
        name="fourier_stage1",
    )(f_view, chan, stage1)

    tk1 = tb
    y_view = y.reshape(2, n2, n1 * w)
    scale = 1.0 / math.sqrt(float(s) * FOURIER_GROUP_DIM)
    est2 = 2 * 2 * n2 * tk1 * w * 2 + 2 * n2 * tk1 * w * 2 + 2 * n2 * tk1 * w * 4
    out = pl.pallas_call(
        functools.partial(_fourier2_body, scale=scale),
        grid=(n1 // tk1,),
        in_specs=[
            pl.BlockSpec((2, n2, tk1 * w), lambda j: (0, 0, j)),
            pl.BlockSpec((n2, 2 * n2), lambda j: (0, 0)),
        ],
        out_specs=pl.BlockSpec((n2, tk1 * w), lambda j: (0, j)),
        out_shape=jax.ShapeDtypeStruct((n2, n1 * w), BF16),
        compiler_params=_params(est2, 1),
        name="fourier_stage2",
    )(y_view, minor)
    return out.reshape(s, w)


def _merge_body(a_ref, f_ref, ga_ref, gf_ref, h_ref, wa_ref, wf_ref, wo_ref, post_g_ref, o_ref):
    ya = jnp.dot(a_ref[...], wa_ref[...], preferred_element_type=F32)
    yf = jnp.dot(f_ref[...], wf_ref[...], preferred_element_type=F32)
    merged = ga_ref[...].astype(F32) * ya + gf_ref[...].astype(F32) * yf
    z = jnp.dot(merged.astype(BF16), wo_ref[...], preferred_element_type=F32)
    o_ref[...] = h_ref[...] + _rms(z, post_g_ref[...])


def _merge(attn, mixed, gates, h, w_attn_o, w_fourier, w_out, post_g, *, tm):
    s, d = h.shape
    resident = pl.Buffered(1)
    est = ((ATTN_WIDTH + FOURIER_WIDTH + d) * d * 2
           + 2 * tm * (ATTN_WIDTH + FOURIER_WIDTH + 2 * d) * 2 + 2 * 2 * tm * d * 4 + 4 * tm * d * 4)
    return pl.pallas_call(
        _merge_body,
        grid=(s // tm,),
        in_specs=[
            pl.BlockSpec((tm, ATTN_WIDTH), lambda i: (i, 0)),
            pl.BlockSpec((tm, FOURIER_WIDTH), lambda i: (i, 0)),
            pl.BlockSpec((tm, d), lambda i: (i, 0)),
            pl.BlockSpec((tm, d), lambda i: (i, 1)),
            pl.BlockSpec((tm, d), lambda i: (i, 0)),
            pl.BlockSpec((ATTN_WIDTH, d), lambda i: (0, 0), pipeline_mode=resident),
            pl.BlockSpec((FOURIER_WIDTH, d), lambda i: (0, 0), pipeline_mode=resident),
            pl.BlockSpec((d, d), lambda i: (0, 0), pipeline_mode=resident),
            pl.BlockSpec((1, d), lambda i: (0, 0)),
        ],
        out_specs=pl.BlockSpec((tm, d), lambda i: (i, 0)),
        out_shape=jax.ShapeDtypeStruct((s, d), F32),
        compiler_params=_params(est, 1),
        name="gated_merge",
    )(attn, mixed, gates, gates, h, w_attn_o, w_fourier, w_out, post_g)


def _tiles(seq):
    return dict(
        ffn_tm=min(512, seq), ffn_tf=512,
        proj_tm=min(1024, seq), proj_tn=1024,
        attn_tq=min(512, seq), attn_tk=min(8192, seq),
        four_tb=8,
        merge_tm=min(256, seq),
    )


def kernel(x, ffn1_pre_g, ffn1_w_gate, ffn1_w_up, ffn1_w_down, ffn1_post_g, mix_pre_g, w_in, b_gate, q_norm_g, k_norm_g, w_attn_o, w_fourier, w_out, mix_post_g, ffn2_pre_g, ffn2_w_gate, ffn2_w_up, ffn2_w_down, ffn2_post_g):
    batch, seq, d = x.shape
    assert batch == 1 and seq % SEQ_DFT_MINOR == 0
    depth = w_in.shape[0]
    t = _tiles(seq)
    cos_tab, sin_tab = _rope_tables(seq)
    chan, stage1, minor = _dft_tables(seq)
    row = lambda v: v.reshape(1, -1).astype(F32)

    h = x.reshape(seq, d)
    for l in range(depth):
        h = _ffn(h, row(ffn1_pre_g[l]), ffn1_w_gate[l].astype(BF16), ffn1_w_up[l].astype(BF16),
                 ffn1_w_down[l].astype(BF16), row(ffn1_post_g[l]), tm=t["ffn_tm"], tf=t["ffn_tf"])

        n_in = w_in.shape[2]
        bias_full = jnp.concatenate([jnp.zeros((1, n_in - 2 * d), F32), row(b_gate[l])], axis=1)
        n_qk = ATTN_WIDTH + KV_WIDTH
        w_proj = w_in[l].astype(BF16)
        w_proj = w_proj.at[:, :n_qk].set(_rope_head_perm(w_proj[:, :n_qk], N_Q_HEADS + N_KV_HEADS))
        q_scale = (HEAD_DIM ** -0.5) * LOG2_E
        q_g2 = jnp.tile(_rope_head_perm(q_norm_g[l].astype(F32), 1) * q_scale, 2).reshape(1, -1)
        k_g2 = jnp.tile(_rope_head_perm(k_norm_g[l].astype(F32), 1), 2).reshape(1, -1)
        q, k, vt, f, gates = _in_proj(h, row(mix_pre_g[l]), w_proj, bias_full, q_g2, k_g2, cos_tab, sin_tab,
                                      tm=t["proj_tm"], tn=t["proj_tn"])

        bounded = (_score_bound(q_norm_g[l], k_norm_g[l]) <= SCORE_BOUND_LIMIT).astype(jnp.int32).reshape(1)
        attn = _attention(bounded, q, k, vt, tq=t["attn_tq"], tk=t["attn_tk"])

        mixed = _fourier(f, chan, stage1, minor, tb=t["four_tb"])

        h = _merge(attn, mixed, gates, h, w_attn_o[l].astype(BF16), w_fourier[l].astype(BF16),
                   w_out[l].astype(BF16), row(mix_post_g[l]), tm=t["merge_tm"])

        h = _ffn(h, row(ffn2_pre_g[l]), ffn2_w_gate[l].astype(BF16), ffn2_w_up[l].astype(BF16),
                 ffn2_w_down[l].astype(BF16), row(ffn2_post_g[l]), tm=t["ffn_tm"], tf=t["ffn_tf"])
    return h.reshape(batch, seq, d)
```

```python
import functools
import math

import jax
import jax.numpy as jnp
from jax import lax
from jax.experimental import pallas as pl
from jax.experimental.pallas import tpu as pltpu

HEAD_DIM = 128
N_KV_HEADS = 4
Q_PER_KV = 4
N_Q_HEADS = N_KV_HEADS * Q_PER_KV
ATTN_WIDTH = N_Q_HEADS * HEAD_DIM
KV_WIDTH = N_KV_HEADS * HEAD_DIM
FOURIER_GROUPS = 8
FOURIER_GROUP_DIM = 128
FOURIER_WIDTH = FOURIER_GROUPS * FOURIER_GROUP_DIM
ROPE_HALF = HEAD_DIM // 4
ROPE_THETA = 10000.0
GRID_W = 64
EPS = 1e-6
MACARON_WEIGHT = 0.5
LOG2_E = 1.4426950408889634

V7X_VMEM_BYTES = 64 * 1024 * 1024
V7X_LANES = 128
SEQ_DFT_MINOR = 128

BF16 = jnp.bfloat16
F32 = jnp.float32


def _vmem_limit(estimate_bytes):
    return int(min(estimate_bytes * 5 // 4 + (4 << 20), V7X_VMEM_BYTES - (6 << 20)))


def _params(estimate_bytes, n_axes):
    return pltpu.CompilerParams(
        dimension_semantics=("arbitrary",) * n_axes,
        vmem_limit_bytes=_vmem_limit(estimate_bytes),
    )


def _rms(x, g):
    ms = jnp.mean(x * x, axis=-1, keepdims=True)
    return x * lax.rsqrt(ms + EPS) * g


def _ffn_body(x_ref, pre_g_ref, wg_ref, wu_ref, wd_ref, post_g_ref, o_ref, xn_ref, inv_ref):
    j = pl.program_id(1)

    @pl.when(j == 0)
    def _():
        xn_ref[...] = _rms(x_ref[...], pre_g_ref[...]).astype(xn_ref.dtype)
        o_ref[...] = jnp.zeros(o_ref.shape, o_ref.dtype)

    xn = xn_ref[...]
    half = wg_ref.shape[1] // 2
    hidden = []
    for lo in (0, half):
        gate = jnp.dot(xn, wg_ref[:, lo:lo + half], preferred_element_type=F32)
        up = jnp.dot(xn, wu_ref[:, lo:lo + half], preferred_element_type=F32)
        hidden.append((gate * jax.nn.sigmoid(gate) * up).astype(BF16))
    o_ref[...] += jnp.dot(jnp.concatenate(hidden, axis=1), wd_ref[...], preferred_element_type=F32)

    last = pl.num_programs(1) - 1

    @pl.when(j == last)
    def _():
        y = o_ref[...]
        inv_ref[...] = lax.rsqrt(jnp.mean(y * y, axis=-1, keepdims=True) + EPS)

    @pl.when(j == last)
    def _():
        o_ref[...] = x_ref[...] + o_ref[...] * inv_ref[...] * post_g_ref[...]


def _ffn(x, pre_g, w_gate, w_up, w_down, post_g, *, tm, tf):
    s, d = x.shape
    post_g = MACARON_WEIGHT * post_g
    d_ff = w_gate.shape[1]
    est = 2 * tm * d * 4 * 2 + tm * d * 2 + 2 * 3 * d * tf * 2 + 3 * tm * tf * 4 + tm * d * 4
    return pl.pallas_call(
        _ffn_body,
        grid=(s // tm, d_ff // tf),
        in_specs=[
            pl.BlockSpec((tm, d), lambda i, j: (i, 0)),
            pl.BlockSpec((1, d), lambda i, j: (0, 0)),
            pl.BlockSpec((d, tf), lambda i, j: (0, j)),
            pl.BlockSpec((d, tf), lambda i, j: (0, j)),
            pl.BlockSpec((tf, d), lambda i, j: (j, 0)),
            pl.BlockSpec((1, d), lambda i, j: (0, 0)),
        ],
        out_specs=pl.BlockSpec((tm, d), lambda i, j: (i, 0)),
        out_shape=jax.ShapeDtypeStruct((s, d), F32),
        scratch_shapes=[pltpu.VMEM((tm, d), BF16), pltpu.VMEM((tm, 1), F32)],
        compiler_params=_params(est, 2),
        name="macaron_ffn",
    )(x, pre_g, w_gate, w_up, w_down, post_g)


PROJ_PIECE = 2 * HEAD_DIM


def _norm_rope_pair(y, head_mean, g2, cos, sin):
    ms = jnp.dot((y * y).astype(BF16), head_mean, preferred_element_type=F32)
    yn = y * lax.rsqrt(ms + EPS) * g2
    outs = []
    for h in range(2):
        yh = yn[:, h * HEAD_DIM:(h + 1) * HEAD_DIM]
        outs.append(yh * cos + pltpu.roll(yh, HEAD_DIM // 2, 1) * sin)
    return outs


def _inproj_body(x_ref, pre_g_ref, w_ref, bias_ref, qg_ref, kg_ref, cos_ref, sin_ref, mean_ref,
                 q_ref, k_ref, vt_ref, f_ref, gate_ref, xn_ref, *, tn):
    j = pl.program_id(1)
    n_q_tiles = ATTN_WIDTH // tn
    kv_tile = n_q_tiles
    f_tile = kv_tile + 1

    @pl.when(j == 0)
    def _():
        xn_ref[...] = _rms(x_ref[...], pre_g_ref[...]).astype(xn_ref.dtype)

    def piece(lo, width=PROJ_PIECE):
        return jnp.dot(xn_ref[...], w_ref[:, lo:lo + width], preferred_element_type=F32)

    def normed_heads(width, g_ref):
        for lo2 in range(0, width, 2 * PROJ_PIECE):
            ys = [piece(lo) for lo in range(lo2, lo2 + 2 * PROJ_PIECE, PROJ_PIECE)]
            for p, y in enumerate(ys):
                outs = _norm_rope_pair(y, mean_ref[...], g_ref[...], cos_ref[...], sin_ref[...])
                for h, out in enumerate(outs):
                    yield (lo2 + p * PROJ_PIECE) // HEAD_DIM + h, out

    @pl.when(j < n_q_tiles)
    def _():
        for head, out in normed_heads(tn, qg_ref):
            q_ref[:, head * HEAD_DIM:(head + 1) * HEAD_DIM] = out.astype(q_ref.dtype)

    @pl.when(j == kv_tile)
    def _():
        for head, out in normed_heads(KV_WIDTH, kg_ref):
            k_ref[head] = out.astype(k_ref.dtype)
        for lo in range(KV_WIDTH, 2 * KV_WIDTH, PROJ_PIECE):
            vt_ref[lo - KV_WIDTH:lo - KV_WIDTH + PROJ_PIECE, :] = piece(lo).astype(vt_ref.dtype).T

    @pl.when(j == f_tile)
    def _():
        for lo in range(0, tn, PROJ_PIECE):
            f_ref[:, lo:lo + PROJ_PIECE] = piece(lo).astype(f_ref.dtype)

    @pl.when(j > f_tile)
    def _():
        for lo in range(0, tn, PROJ_PIECE):
            z = piece(lo) + bias_ref[:, lo:lo + PROJ_PIECE]
            gate_ref[:, lo:lo + PROJ_PIECE] = (0.5 * jnp.tanh(0.5 * z) + 0.5).astype(gate_ref.dtype)


def _in_proj(h, pre_g, w_in, bias_full, q_g2, k_g2, cos_tab, sin_tab, *, tm, tn):
    s, d = h.shape
    n = w_in.shape[1]
    assert tn == 2 * KV_WIDTH == FOURIER_WIDTH and ATTN_WIDTH % tn == 0
    n_q_tiles = ATTN_WIDTH // tn
    gate_tile0 = n_q_tiles + 2
    n_gate_tiles = n // tn - gate_tile0
    lane_head = jnp.arange(PROJ_PIECE, dtype=jnp.int32) // HEAD_DIM
    head_mean = jnp.where(lane_head[:, None] == lane_head[None, :], 1.0 / HEAD_DIM, 0.0).astype(BF16)
    est = (2 * tm * d * 4 + tm * d * 2 + 2 * d * tn * 2 + 2 * 2 * tm * tn * 2 * 2 + 2 * 2 * tm * KV_WIDTH * 2
           + 4 * tm * PROJ_PIECE * 4 + 4 * tm * HEAD_DIM * 4)
    return pl.pallas_call(
        functools.partial(_inproj_body, tn=tn),
        grid=(s // tm, n // tn),
        in_specs=[
            pl.BlockSpec((tm, d), lambda i, j: (i, 0)),
            pl.BlockSpec((1, d), lambda i, j: (0, 0)),
            pl.BlockSpec((d, tn), lambda i, j: (0, j)),
            pl.BlockSpec((1, tn), lambda i, j: (0, j)),
            pl.BlockSpec((1, PROJ_PIECE), lambda i, j: (0, 0)),
            pl.BlockSpec((1, PROJ_PIECE), lambda i, j: (0, 0)),
            pl.BlockSpec((tm, HEAD_DIM), lambda i, j: (i, 0)),
            pl.BlockSpec((tm, HEAD_DIM), lambda i, j: (i, 0)),
            pl.BlockSpec((PROJ_PIECE, PROJ_PIECE), lambda i, j: (0, 0)),
        ],
        out_specs=[
            pl.BlockSpec((tm, tn), lambda i, j: (i, jnp.minimum(j, n_q_tiles - 1))),
            pl.BlockSpec((N_KV_HEADS, tm, HEAD_DIM), lambda i, j: (0, i, 0)),
            pl.BlockSpec((KV_WIDTH, tm), lambda i, j: (0, i)),
            pl.BlockSpec((tm, FOURIER_WIDTH), lambda i, j: (i, 0)),
            pl.BlockSpec((tm, tn), lambda i, j: (i, jnp.clip(j - gate_tile0, 0, n_gate_tiles - 1))),
        ],
        out_shape=[
            jax.ShapeDtypeStruct((s, ATTN_WIDTH), BF16),
            jax.ShapeDtypeStruct((N_KV_HEADS, s, HEAD_DIM), BF16),
            jax.ShapeDtypeStruct((KV_WIDTH, s), BF16),
            jax.ShapeDtypeStruct((s, FOURIER_WIDTH), BF16),
            jax.ShapeDtypeStruct((s, n_gate_tiles * tn), BF16),
        ],
        scratch_shapes=[pltpu.VMEM((tm, d), BF16)],
        compiler_params=_params(est, 2),
        name="in_proj",
    )(h, pre_g, w_in, bias_full, q_g2, k_g2, cos_tab, sin_tab, head_mean)


def _rope_head_perm(v, n_heads):
    lead = v.shape[:-1]
    v = v.reshape(lead + (n_heads, 2, 2, ROPE_HALF))
    return jnp.swapaxes(v, -3, -2).reshape(lead + (n_heads * HEAD_DIM,))


def _rope_tables(seq):
    n_rows = seq // GRID_W
    n_freq = ROPE_HALF
    inv_freq = ROPE_THETA ** (-jnp.arange(n_freq, dtype=F32) / n_freq)
    ang_r = jnp.arange(n_rows, dtype=F32)[:, None] * inv_freq
    ang_c = jnp.arange(GRID_W, dtype=F32)[:, None] * inv_freq
    by_row = lambda v: jnp.repeat(v, GRID_W, axis=0)
    by_col = lambda v: jnp.tile(v, (n_rows, 1))
    cos_r, sin_r = by_row(jnp.cos(ang_r)), by_row(jnp.sin(ang_r))
    cos_c, sin_c = by_col(jnp.cos(ang_c)), by_col(jnp.sin(ang_c))
    cos_tab = jnp.concatenate([cos_r, cos_c, cos_r, cos_c], axis=-1)
    sin_tab = jnp.concatenate([-sin_r, -sin_c, sin_r, sin_c], axis=-1)
    return cos_tab, sin_tab


SCORE_BOUND_LIMIT = 100.0


def _kv_chunks(k_ref, vt_ref, tk):
    for c in range(k_ref.shape[0] // tk):
        yield k_ref[c * tk:(c + 1) * tk, :], vt_ref[:, c * tk:(c + 1) * tk]


def _scores_t(k_c, q_ref, h):
    q_h = q_ref[:, h * HEAD_DIM:(h + 1) * HEAD_DIM]
    return lax.dot_general(k_c, q_h, (((1,), (1,)), ((), ())), preferred_element_type=F32)


def _attn_finish(o_ref, l_ref, acc_ref):
    for h in range(Q_PER_KV):
        o_t = acc_ref[h] / l_ref[h]
        o_ref[:, h * HEAD_DIM:(h + 1) * HEAD_DIM] = o_t.T.astype(o_ref.dtype)


def _attn_body_bounded(q_ref, k_ref, vt_ref, o_ref, l_ref, acc_ref, *, tk):
    l_ref[...] = jnp.zeros(l_ref.shape, F32)
    acc_ref[...] = jnp.zeros(acc_ref.shape, F32)
    for k_c, vt_c in _kv_chunks(k_ref, vt_ref, tk):
        s_next = _scores_t(k_c, q_ref, 0)
        for h in range(Q_PER_KV):
            s_t = s_next
            if h + 1 < Q_PER_KV:
                s_next = _scores_t(k_c, q_ref, h + 1)
            p_t = jnp.exp2(s_t)
            l_ref[h] += jnp.sum(p_t, axis=0, keepdims=True)
            acc_ref[h] += jnp.dot(vt_c, p_t.astype(BF16), preferred_element_type=F32)
    _attn_finish(o_ref, l_ref, acc_ref)


def _attn_body_online(q_ref, k_ref, vt_ref, o_ref, m_ref, l_ref, acc_ref, *, tk):
    m_ref[...] = jnp.full(m_ref.shape, -jnp.inf, F32)
    l_ref[...] = jnp.zeros(l_ref.shape, F32)
    acc_ref[...] = jnp.zeros(acc_ref.shape, F32)
    for k_c, vt_c in _kv_chunks(k_ref, vt_ref, tk):
        for h in range(Q_PER_KV):
            s_t = _scores_t(k_c, q_ref, h)
            m_old = m_ref[h]
            m_new = jnp.maximum(m_old, jnp.max(s_t, axis=0, keepdims=True))
            alpha = jnp.exp2(m_old - m_new)
            p_t = jnp.exp2(s_t - m_new)
            l_ref[h] = alpha * l_ref[h] + jnp.sum(p_t, axis=0, keepdims=True)
            acc_ref[h] = alpha * acc_ref[h] + jnp.dot(vt_c, p_t.astype(BF16), preferred_element_type=F32)
            m_ref[h] = m_new
    _attn_finish(o_ref, l_ref, acc_ref)


def _attention(q, k, vt, *, tq, tk, bounded):
    s = q.shape[0]
    group_w = Q_PER_KV * HEAD_DIM
    est = (2 * tq * group_w * 2 * 2 + 2 * s * 2 * HEAD_DIM * 2 + Q_PER_KV * (HEAD_DIM + 16) * tq * 4
           + 4 * tk * tq * 4)
    stat = pltpu.VMEM((Q_PER_KV, 1, tq), F32)
    acc = pltpu.VMEM((Q_PER_KV, HEAD_DIM, tq), F32)
    if bounded:
        body, name, scratch = _attn_body_bounded, "gqa_attention_bounded", [stat, acc]
    else:
        body, name, scratch = _attn_body_online, "gqa_attention_online", [stat, stat, acc]
    return pl.pallas_call(
        functools.partial(body, tk=tk),
        grid=(N_KV_HEADS, s // tq),
        in_specs=[
            pl.BlockSpec((tq, group_w), lambda g, i: (i, g)),
            pl.BlockSpec((None, s, HEAD_DIM), lambda g, i: (g, 0, 0)),
            pl.BlockSpec((HEAD_DIM, s), lambda g, i: (g, 0)),
        ],
        out_specs=pl.BlockSpec((tq, group_w), lambda g, i: (i, g)),
        out_shape=jax.ShapeDtypeStruct((s, ATTN_WIDTH), BF16),
        scratch_shapes=scratch,
        compiler_params=_params(est, 2),
        name=name,
    )(q, k, vt)


def _score_bound(q_g, k_g):
    q_scale = (HEAD_DIM ** -0.5) * LOG2_E
    return HEAD_DIM * q_scale * jnp.max(jnp.abs(q_g)) * jnp.max(jnp.abs(k_g))


def _fourier1_body(f_ref, chan_ref, t_ref, y_ref, ab_ref, *, tb):
    n1 = f_ref.shape[0]
    for b in range(tb):
        for g in range(FOURIER_GROUPS):
            lo = b * FOURIER_WIDTH + g * FOURIER_GROUP_DIM
            ab = jnp.dot(f_ref[:, lo:lo + FOURIER_GROUP_DIM], chan_ref[...], preferred_element_type=F32)
            cols = slice(g * FOURIER_GROUP_DIM, (g + 1) * FOURIER_GROUP_DIM)
            ab_ref[:n1, cols] = ab[:, :FOURIER_GROUP_DIM].astype(ab_ref.dtype)
            ab_ref[n1:, cols] = ab[:, FOURIER_GROUP_DIM:].astype(ab_ref.dtype)
        y = jnp.dot(t_ref[b], ab_ref[...], preferred_element_type=F32)
        y_ref[0, b] = y[:n1].astype(y_ref.dtype)
        y_ref[1, b] = y[n1:].astype(y_ref.dtype)


def _fourier2_body(y_ref, f2_ref, o_ref, *, scale):
    two, n2, w = y_ref.shape
    y = y_ref[...].reshape(two * n2, w)
    o_ref[...] = (jnp.dot(f2_ref[...], y, preferred_element_type=F32) * scale).astype(o_ref.dtype)


def _dft_tables(seq):
    n2 = SEQ_DFT_MINOR
    n1 = seq // n2
    idx = jnp.arange(n2, dtype=jnp.int32)
    ang = (2.0 * math.pi / n2) * ((idx[:, None] * idx[None, :]) % n2).astype(F32)
    cm, sm = jnp.cos(ang), jnp.sin(ang)
    chan = jnp.concatenate([cm, sm], axis=1).astype(BF16)
    minor = jnp.concatenate([cm, sm], axis=1).astype(BF16)
    k1 = jnp.arange(n1, dtype=jnp.int32)
    a1 = (2.0 * math.pi / n1) * ((k1[:, None] * k1[None, :]) % n1).astype(F32)
    a2 = (2.0 * math.pi / seq) * (jnp.arange(n2, dtype=jnp.int32)[:, None] * k1[None, :]).astype(F32)
    c1, s1 = jnp.cos(a1)[None], jnp.sin(a1)[None]
    c2, s2 = jnp.cos(a2)[:, :, None], jnp.sin(a2)[:, :, None]
    tc = c1 * c2 - s1 * s2
    ts = s1 * c2 + c1 * s2
    top = jnp.concatenate([tc, -ts], axis=2)
    bot = jnp.concatenate([-ts, -tc], axis=2)
    stage1 = jnp.concatenate([top, bot], axis=1).astype(BF16)
    return chan, stage1, minor


def _fourier(f, chan, stage1, minor, *, tb):
    s, w = f.shape
    n2 = SEQ_DFT_MINOR
    n1 = s // n2
    f_view = f.reshape(n1, n2 * w)
    est1 = 2 * n1 * tb * w * 2 + 2 * tb * 4 * n1 * n1 * 2 + 2 * 2 * tb * n1 * w * 2 + 2 * n1 * w * 2 + 4 * n1 * w * 4
    y = pl.pallas_call(
        functools.partial(_fourier1_body, tb=tb),
        grid=(n2 // tb,),
        in_specs=[
            pl.BlockSpec((n1, tb * w), lambda j: (0, j)),
            pl.BlockSpec((FOURIER_GROUP_DIM, 2 * FOURIER_GROUP_DIM), lambda j: (0, 0)),
            pl.BlockSpec((tb, 2 * n1, 2 * n1), lambda j: (j, 0, 0)),
        ],
        out_specs=pl.BlockSpec((2, tb, n1, w), lambda j: (0, j, 0, 0)),
        out_shape=jax.ShapeDtypeStruct((2, n2, n1, w), BF16),
        scratch_shapes=[pltpu.VMEM((2 * n1, w), BF16)],
        compiler_params=_params(est1, 1),
        name="fourier_stage1",
    )(f_view, chan, stage1)

    tk1 = tb
    y_view = y.reshape(2, n2, n1 * w)
    scale = 1.0 / math.sqrt(float(s) * FOURIER_GROUP_DIM)
    est2 = 2 * 2 * n2 * tk1 * w * 2 + 2 * n2 * tk1 * w * 2 + 2 * n2 * tk1 * w * 4
    out = pl.pallas_call(
        functools.partial(_fourier2_body, scale=scale),
        grid=(n1 // tk1,),
        in_specs=[
            pl.BlockSpec((2, n2, tk1 * w), lambda j: (0, 0, j)),
            pl.BlockSpec((n2, 2 * n2), lambda j: (0, 0)),
        ],
        out_specs=pl.BlockSpec((n2, tk1 * w), lambda j: (0, j)),
        out_shape=jax.ShapeDtypeStruct((n2, n1 * w), BF16),
        compiler_params=_params(est2, 1),
        name="fourier_stage2",
    )(y_view, minor)
    return out.reshape(s, w)


def _merge_body(a_ref, f_ref, ga_ref, gf_ref, h_ref, wa_ref, wf_ref, wo_ref, post_g_ref, o_ref):
    ya = jnp.dot(a_ref[...], wa_ref[...], preferred_element_type=F32)
    yf = jnp.dot(f_ref[...], wf_ref[...], preferred_element_type=F32)
    merged = ga_ref[...].astype(F32) * ya + gf_ref[...].astype(F32) * yf
    z = jnp.dot(merged.astype(BF16), wo_ref[...], preferred_element_type=F32)
    o_ref[...] = h_ref[...] + _rms(z, post_g_ref[...])


def _merge(attn, mixed, gates, h, w_attn_o, w_fourier, w_out, post_g, *, tm):
    s, d = h.shape
    resident = pl.Buffered(1)
    est = ((ATTN_WIDTH + FOURIER_WIDTH + d) * d * 2
           + 2 * tm * (ATTN_WIDTH + FOURIER_WIDTH + 2 * d) * 2 + 2 * 2 * tm * d * 4 + 4 * tm * d * 4)
    return pl.pallas_call(
        _merge_body,
        grid=(s // tm,),
        in_specs=[
            pl.BlockSpec((tm, ATTN_WIDTH), lambda i: (i, 0)),
            pl.BlockSpec((tm, FOURIER_WIDTH), lambda i: (i, 0)),
            pl.BlockSpec((tm, d), lambda i: (i, 0)),
            pl.BlockSpec((tm, d), lambda i: (i, 1)),
            pl.BlockSpec((tm, d), lambda i: (i, 0)),
            pl.BlockSpec((ATTN_WIDTH, d), lambda i: (0, 0), pipeline_mode=resident),
            pl.BlockSpec((FOURIER_WIDTH, d), lambda i: (0, 0), pipeline_mode=resident),
            pl.BlockSpec((d, d), lambda i: (0, 0), pipeline_mode=resident),
            pl.BlockSpec((1, d), lambda i: (0, 0)),
        ],
        out_specs=pl.BlockSpec((tm, d), lambda i: (i, 0)),
        out_shape=jax.ShapeDtypeStruct((s, d), F32),
        compiler_params=_params(est, 1),
        name="gated_merge",
    )(attn, mixed, gates, gates, h, w_attn_o, w_fourier, w_out, post_g)


def _tiles(seq):
    return dict(
        ffn_tm=min(512, seq), ffn_tf=512,
        proj_tm=min(1024, seq), proj_tn=1024,
        attn_tq=min(512, seq), attn_tk=min(8192, seq),
        four_tb=8,
        merge_tm=min(256, seq),
    )


def kernel(x, ffn1_pre_g, ffn1_w_gate, ffn1_w_up, ffn1_w_down, ffn1_post_g, mix_pre_g, w_in, b_gate, q_norm_g, k_norm_g, w_attn_o, w_fourier, w_out, mix_post_g, ffn2_pre_g, ffn2_w_gate, ffn2_w_up, ffn2_w_down, ffn2_post_g):
    batch, seq, d = x.shape
    assert batch == 1 and seq % SEQ_DFT_MINOR == 0
    depth = w_in.shape[0]
    t = _tiles(seq)
    cos_tab, sin_tab = _rope_tables(seq)
    chan, stage1, minor = _dft_tables(seq)
    row = lambda v: v.reshape(1, -1).astype(F32)

    h = x.reshape(seq, d)
    for l in range(depth):
        h = _ffn(h, row(ffn1_pre_g[l]), ffn1_w_gate[l].astype(BF16), ffn1_w_up[l].astype(BF16),
                 ffn1_w_down[l].astype(BF16), row(ffn1_post_g[l]), tm=t["ffn_tm"], tf=t["ffn_tf"])

        n_in = w_in.shape[2]
        bias_full = jnp.concatenate([jnp.zeros((1, n_in - 2 * d), F32), row(b_gate[l])], axis=1)
        n_qk = ATTN_WIDTH + KV_WIDTH
        w_proj = w_in[l].astype(BF16)
        w_proj = w_proj.at[:, :n_qk].set(_rope_head_perm(w_proj[:, :n_qk], N_Q_HEADS + N_KV_HEADS))
        q_scale = (HEAD_DIM ** -0.5) * LOG2_E
        q_g2 = jnp.tile(_rope_head_perm(q_norm_g[l].astype(F32), 1) * q_scale, 2).reshape(1, -1)
        k_g2 = jnp.tile(_rope_head_perm(k_norm_g[l].astype(F32), 1), 2).reshape(1, -1)
        q, k, vt, f, gates = _in_proj(h, row(mix_pre_g[l]), w_proj, bias_full, q_g2, k_g2, cos_tab, sin_tab,
                                      tm=t["proj_tm"], tn=t["proj_tn"])

        attn = lax.cond(
            _score_bound(q_norm_g[l], k_norm_g[l]) <= SCORE_BOUND_LIMIT,
            functools.partial(_attention, tq=t["attn_tq"], tk=t["attn_tk"], bounded=True),
            functools.partial(_attention, tq=t["attn_tq"], tk=t["attn_tk"], bounded=False),
            q, k, vt)

        mixed = _fourier(f, chan, stage1, minor, tb=t["four_tb"])

        h = _merge(attn, mixed, gates, h, w_attn_o[l].astype(BF16), w_fourier[l].astype(BF16),
                   w_out[l].astype(BF16), row(mix_post_g[l]), tm=t["merge_tm"])

        h = _ffn(h, row(ffn2_pre_g[l]), ffn2_w_gate[l].astype(BF16), ffn2_w_up[l].astype(BF16),
                 ffn2_w_down[l].astype(BF16), row(ffn2_post_g[l]), tm=t["ffn_tm"], tf=t["ffn_tf"])
    return h.reshape(batch, seq, d)
```

```python
import functools
import math

import jax
import jax.numpy as jnp
from jax import lax
from jax.experimental import pallas as pl
from jax.experimental.pallas import tpu as pltpu

HEAD_DIM = 128
N_KV_HEADS = 4
Q_PER_KV = 4
N_Q_HEADS = N_KV_HEADS * Q_PER_KV
ATTN_WIDTH = N_Q_HEADS * HEAD_DIM
KV_WIDTH = N_KV_HEADS * HEAD_DIM
FOURIER_GROUPS = 8
FOURIER_GROUP_DIM = 128
FOURIER_WIDTH = FOURIER_GROUPS * FOURIER_GROUP_DIM
ROPE_HALF = HEAD_DIM // 4
ROPE_THETA = 10000.0
GRID_W = 64
EPS = 1e-6
MACARON_WEIGHT = 0.5
LOG2_E = 1.4426950408889634

V7X_VMEM_BYTES = 64 * 1024 * 1024
V7X_LANES = 128
SEQ_DFT_MINOR = 128

BF16 = jnp.bfloat16
F32 = jnp.float32


def _vmem_limit(estimate_bytes):
    return int(min(estimate_bytes * 5 // 4 + (4 << 20), V7X_VMEM_BYTES - (6 << 20)))


def _params(estimate_bytes, n_axes):
    return pltpu.CompilerParams(
        dimension_semantics=("arbitrary",) * n_axes,
        vmem_limit_bytes=_vmem_limit(estimate_bytes),
    )


def _rms(x, g):
    ms = jnp.mean(x * x, axis=-1, keepdims=True)
    return x * lax.rsqrt(ms + EPS) * g


def _ffn_body(x_ref, pre_g_ref, wg_ref, wu_ref, wd_ref, post_g_ref, o_ref, xn_ref, inv_ref):
    j = pl.program_id(1)

    @pl.when(j == 0)
    def _():
        xn_ref[...] = _rms(x_ref[...], pre_g_ref[...]).astype(xn_ref.dtype)
        o_ref[...] = jnp.zeros(o_ref.shape, o_ref.dtype)

    xn = xn_ref[...]
    half = wg_ref.shape[1] // 2
    hidden = []
    for lo in (0, half):
        gate = jnp.dot(xn, wg_ref[:, lo:lo + half], preferred_element_type=F32)
        up = jnp.dot(xn, wu_ref[:, lo:lo + half], preferred_element_type=F32)
        hidden.append((gate * jax.nn.sigmoid(gate) * up).astype(BF16))
    o_ref[...] += jnp.dot(jnp.concatenate(hidden, axis=1), wd_ref[...], preferred_element_type=F32)

    last = pl.num_programs(1) - 1

    @pl.when(j == last)
    def _():
        y = o_ref[...]
        inv_ref[...] = lax.rsqrt(jnp.mean(y * y, axis=-1, keepdims=True) + EPS)

    @pl.when(j == last)
    def _():
        o_ref[...] = x_ref[...] + o_ref[...] * inv_ref[...] * post_g_ref[...]


def _ffn(x, pre_g, w_gate, w_up, w_down, post_g, *, tm, tf):
    s, d = x.shape
    post_g = MACARON_WEIGHT * post_g
    d_ff = w_gate.shape[1]
    est = 2 * tm * d * 4 * 2 + tm * d * 2 + 2 * 3 * d * tf * 2 + 3 * tm * tf * 4 + tm * d * 4
    return pl.pallas_call(
        _ffn_body,
        grid=(s // tm, d_ff // tf),
        in_specs=[
            pl.BlockSpec((tm, d), lambda i, j: (i, 0)),
            pl.BlockSpec((1, d), lambda i, j: (0, 0)),
            pl.BlockSpec((d, tf), lambda i, j: (0, j)),
            pl.BlockSpec((d, tf), lambda i, j: (0, j)),
            pl.BlockSpec((tf, d), lambda i, j: (j, 0)),
            pl.BlockSpec((1, d), lambda i, j: (0, 0)),
        ],
        out_specs=pl.BlockSpec((tm, d), lambda i, j: (i, 0)),
        out_shape=jax.ShapeDtypeStruct((s, d), F32),
        scratch_shapes=[pltpu.VMEM((tm, d), BF16), pltpu.VMEM((tm, 1), F32)],
        compiler_params=_params(est, 2),
        name="macaron_ffn",
    )(x, pre_g, w_gate, w_up, w_down, post_g)


PROJ_PIECE = 2 * HEAD_DIM


def _norm_rope_pair(y, head_mean, g2, cos, sin):
    ms = jnp.dot((y * y).astype(BF16), head_mean, preferred_element_type=F32)
    yn = y * lax.rsqrt(ms + EPS) * g2
    outs = []
    for h in range(2):
        yh = yn[:, h * HEAD_DIM:(h + 1) * HEAD_DIM]
        outs.append(yh * cos + pltpu.roll(yh, HEAD_DIM // 2, 1) * sin)
    return outs


def _inproj_body(x_ref, pre_g_ref, w_ref, bias_ref, qg_ref, kg_ref, cos_ref, sin_ref, mean_ref,
                 q_ref, k_ref, vt_ref, f_ref, gate_ref, xn_ref, *, tn):
    j = pl.program_id(1)
    n_q_tiles = ATTN_WIDTH // tn
    kv_tile = n_q_tiles
    f_tile = kv_tile + 1

    @pl.when(j == 0)
    def _():
        xn_ref[...] = _rms(x_ref[...], pre_g_ref[...]).astype(xn_ref.dtype)

    def piece(lo, width=PROJ_PIECE):
        return jnp.dot(xn_ref[...], w_ref[:, lo:lo + width], preferred_element_type=F32)

    def normed_heads(width, g_ref):
        for lo2 in range(0, width, 2 * PROJ_PIECE):
            ys = [piece(lo) for lo in range(lo2, lo2 + 2 * PROJ_PIECE, PROJ_PIECE)]
            for p, y in enumerate(ys):
                outs = _norm_rope_pair(y, mean_ref[...], g_ref[...], cos_ref[...], sin_ref[...])
                for h, out in enumerate(outs):
                    yield (lo2 + p * PROJ_PIECE) // HEAD_DIM + h, out

    @pl.when(j < n_q_tiles)
    def _():
        for head, out in normed_heads(tn, qg_ref):
            q_ref[:, head * HEAD_DIM:(head + 1) * HEAD_DIM] = out.astype(q_ref.dtype)

    @pl.when(j == kv_tile)
    def _():
        for head, out in normed_heads(KV_WIDTH, kg_ref):
            k_ref[head] = out.astype(k_ref.dtype)
        for lo in range(KV_WIDTH, 2 * KV_WIDTH, PROJ_PIECE):
            vt_ref[lo - KV_WIDTH:lo - KV_WIDTH + PROJ_PIECE, :] = piece(lo).astype(vt_ref.dtype).T

    @pl.when(j == f_tile)
    def _():
        for lo in range(0, tn, PROJ_PIECE):
            f_ref[:, lo:lo + PROJ_PIECE] = piece(lo).astype(f_ref.dtype)

    @pl.when(j > f_tile)
    def _():
        for lo in range(0, tn, PROJ_PIECE):
            z = piece(lo) + bias_ref[:, lo:lo + PROJ_PIECE]
            gate_ref[:, lo:lo + PROJ_PIECE] = (0.5 * jnp.tanh(0.5 * z) + 0.5).astype(gate_ref.dtype)


def _in_proj(h, pre_g, w_in, bias_full, q_g2, k_g2, cos_tab, sin_tab, *, tm, tn):
    s, d = h.shape
    n = w_in.shape[1]
    assert tn == 2 * KV_WIDTH == FOURIER_WIDTH and ATTN_WIDTH % tn == 0
    n_q_tiles = ATTN_WIDTH // tn
    gate_tile0 = n_q_tiles + 2
    n_gate_tiles = n // tn - gate_tile0
    lane_head = jnp.arange(PROJ_PIECE, dtype=jnp.int32) // HEAD_DIM
    head_mean = jnp.where(lane_head[:, None] == lane_head[None, :], 1.0 / HEAD_DIM, 0.0).astype(BF16)
    est = (2 * tm * d * 4 + tm * d * 2 + 2 * d * tn * 2 + 2 * 2 * tm * tn * 2 * 2 + 2 * 2 * tm * KV_WIDTH * 2
           + 4 * tm * PROJ_PIECE * 4 + 4 * tm * HEAD_DIM * 4)
    return pl.pallas_call(
        functools.partial(_inproj_body, tn=tn),
        grid=(s // tm, n // tn),
        in_specs=[
            pl.BlockSpec((tm, d), lambda i, j: (i, 0)),
            pl.BlockSpec((1, d), lambda i, j: (0, 0)),
            pl.BlockSpec((d, tn), lambda i, j: (0, j)),
            pl.BlockSpec((1, tn), lambda i, j: (0, j)),
            pl.BlockSpec((1, PROJ_PIECE), lambda i, j: (0, 0)),
            pl.BlockSpec((1, PROJ_PIECE), lambda i, j: (0, 0)),
            pl.BlockSpec((tm, HEAD_DIM), lambda i, j: (i, 0)),
            pl.BlockSpec((tm, HEAD_DIM), lambda i, j: (i, 0)),
            pl.BlockSpec((PROJ_PIECE, PROJ_PIECE), lambda i, j: (0, 0)),
        ],
        out_specs=[
            pl.BlockSpec((tm, tn), lambda i, j: (i, jnp.minimum(j, n_q_tiles - 1))),
            pl.BlockSpec((N_KV_HEADS, tm, HEAD_DIM), lambda i, j: (0, i, 0)),
            pl.BlockSpec((KV_WIDTH, tm), lambda i, j: (0, i)),
            pl.BlockSpec((tm, FOURIER_WIDTH), lambda i, j: (i, 0)),
            pl.BlockSpec((tm, tn), lambda i, j: (i, jnp.clip(j - gate_tile0, 0, n_gate_tiles - 1))),
        ],
        out_shape=[
            jax.ShapeDtypeStruct((s, ATTN_WIDTH), BF16),
            jax.ShapeDtypeStruct((N_KV_HEADS, s, HEAD_DIM), BF16),
            jax.ShapeDtypeStruct((KV_WIDTH, s), BF16),
            jax.ShapeDtypeStruct((s, FOURIER_WIDTH), BF16),
            jax.ShapeDtypeStruct((s, n_gate_tiles * tn), BF16),
        ],
        scratch_shapes=[pltpu.VMEM((tm, d), BF16)],
        compiler_params=_params(est, 2),
        name="in_proj",
    )(h, pre_g, w_in, bias_full, q_g2, k_g2, cos_tab, sin_tab, head_mean)


def _rope_head_perm(v, n_heads):
    lead = v.shape[:-1]
    v = v.reshape(lead + (n_heads, 2, 2, ROPE_HALF))
    return jnp.swapaxes(v, -3, -2).reshape(lead + (n_heads * HEAD_DIM,))


def _rope_tables(seq):
    n_rows = seq // GRID_W
    n_freq = ROPE_HALF
    inv_freq = ROPE_THETA ** (-jnp.arange(n_freq, dtype=F32) / n_freq)
    ang_r = jnp.arange(n_rows, dtype=F32)[:, None] * inv_freq
    ang_c = jnp.arange(GRID_W, dtype=F32)[:, None] * inv_freq
    zr, zc = jnp.zeros_like(ang_r), jnp.zeros_like(ang_c)
    def table(row_part, col_part, sign):
        r = jnp.concatenate([sign * row_part, zr, row_part, zr], axis=-1)
        c = jnp.concatenate([zc, sign * col_part, zc, col_part], axis=-1)
        return (r[:, None, :] + c[None, :, :]).reshape(seq, HEAD_DIM)
    cos_tab = table(jnp.cos(ang_r), jnp.cos(ang_c), 1.0)
    sin_tab = table(jnp.sin(ang_r), jnp.sin(ang_c), -1.0)
    return cos_tab, sin_tab


SCORE_BOUND_LIMIT = 100.0


def _kv_chunks(k_ref, vt_ref, tk):
    for c in range(k_ref.shape[0] // tk):
        yield k_ref[c * tk:(c + 1) * tk, :], vt_ref[:, c * tk:(c + 1) * tk]


def _scores_t(k_c, q_ref, h):
    q_h = q_ref[:, h * HEAD_DIM:(h + 1) * HEAD_DIM]
    return lax.dot_general(k_c, q_h, (((1,), (1,)), ((), ())), preferred_element_type=F32)


def _attn_finish(o_ref, l_ref, acc_ref):
    for h in range(Q_PER_KV):
        o_t = acc_ref[h] / l_ref[h]
        o_ref[:, h * HEAD_DIM:(h + 1) * HEAD_DIM] = o_t.T.astype(o_ref.dtype)


def _attn_body_bounded(q_ref, k_ref, vt_ref, o_ref, l_ref, acc_ref, *, tk):
    l_ref[...] = jnp.zeros(l_ref.shape, F32)
    acc_ref[...] = jnp.zeros(acc_ref.shape, F32)
    for k_c, vt_c in _kv_chunks(k_ref, vt_ref, tk):
        s_next = _scores_t(k_c, q_ref, 0)
        for h in range(Q_PER_KV):
            s_t = s_next
            if h + 1 < Q_PER_KV:
                s_next = _scores_t(k_c, q_ref, h + 1)
            p_t = jnp.exp2(s_t)
            l_ref[h] += jnp.sum(p_t, axis=0, keepdims=True)
            acc_ref[h] += jnp.dot(vt_c, p_t.astype(BF16), preferred_element_type=F32)
    _attn_finish(o_ref, l_ref, acc_ref)


def _attn_body_online(q_ref, k_ref, vt_ref, o_ref, m_ref, l_ref, acc_ref, *, tk):
    m_ref[...] = jnp.full(m_ref.shape, -jnp.inf, F32)
    l_ref[...] = jnp.zeros(l_ref.shape, F32)
    acc_ref[...] = jnp.zeros(acc_ref.shape, F32)
    for k_c, vt_c in _kv_chunks(k_ref, vt_ref, tk):
        for h in range(Q_PER_KV):
            s_t = _scores_t(k_c, q_ref, h)
            m_old = m_ref[h]
            m_new = jnp.maximum(m_old, jnp.max(s_t, axis=0, keepdims=True))
            alpha = jnp.exp2(m_old - m_new)
            p_t = jnp.exp2(s_t - m_new)
            l_ref[h] = alpha * l_ref[h] + jnp.sum(p_t, axis=0, keepdims=True)
            acc_ref[h] = alpha * acc_ref[h] + jnp.dot(vt_c, p_t.astype(BF16), preferred_element_type=F32)
            m_ref[h] = m_new
    _attn_finish(o_ref, l_ref, acc_ref)


def _attention(q, k, vt, *, tq, tk, bounded):
    s = q.shape[0]
    group_w = Q_PER_KV * HEAD_DIM
    est = (2 * tq * group_w * 2 * 2 + 2 * s * 2 * HEAD_DIM * 2 + Q_PER_KV * (HEAD_DIM + 16) * tq * 4
           + 4 * tk * tq * 4)
    stat = pltpu.VMEM((Q_PER_KV, 1, tq), F32)
    acc = pltpu.VMEM((Q_PER_KV, HEAD_DIM, tq), F32)
    if bounded:
        body, name, scratch = _attn_body_bounded, "gqa_attention_bounded", [stat, acc]
    else:
        body, name, scratch = _attn_body_online, "gqa_attention_online", [stat, stat, acc]
    return pl.pallas_call(
        functools.partial(body, tk=tk),
        grid=(N_KV_HEADS, s // tq),
        in_specs=[
            pl.BlockSpec((tq, group_w), lambda g, i: (i, g)),
            pl.BlockSpec((None, s, HEAD_DIM), lambda g, i: (g, 0, 0)),
            pl.BlockSpec((HEAD_DIM, s), lambda g, i: (g, 0)),
        ],
        out_specs=pl.BlockSpec((tq, group_w), lambda g, i: (i, g)),
        out_shape=jax.ShapeDtypeStruct((s, ATTN_WIDTH), BF16),
        scratch_shapes=scratch,
        compiler_params=_params(est, 2),
        name=name,
    )(q, k, vt)


def _score_bound(q_g, k_g):
    q_scale = (HEAD_DIM ** -0.5) * LOG2_E
    return HEAD_DIM * q_scale * jnp.max(jnp.abs(q_g)) * jnp.max(jnp.abs(k_g))


def _fourier1_body(f_ref, chan_ref, t_ref, y_ref, ab_ref, *, tb):
    n1 = f_ref.shape[0]
    for b in range(tb):
        for g in range(FOURIER_GROUPS):
            lo = b * FOURIER_WIDTH + g * FOURIER_GROUP_DIM
            ab = jnp.dot(f_ref[:, lo:lo + FOURIER_GROUP_DIM], chan_ref[...], preferred_element_type=F32)
            cols = slice(g * FOURIER_GROUP_DIM, (g + 1) * FOURIER_GROUP_DIM)
            ab_ref[:n1, cols] = ab[:, :FOURIER_GROUP_DIM].astype(ab_ref.dtype)
            ab_ref[n1:, cols] = ab[:, FOURIER_GROUP_DIM:].astype(ab_ref.dtype)
        y = jnp.dot(t_ref[b], ab_ref[...], preferred_element_type=F32)
        y_ref[0, b] = y[:n1].astype(y_ref.dtype)
        y_ref[1, b] = y[n1:].astype(y_ref.dtype)


def _fourier2_body(y_ref, f2_ref, o_ref, *, scale):
    two, n2, w = y_ref.shape
    y = y_ref[...].reshape(two * n2, w)
    o_ref[...] = (jnp.dot(f2_ref[...], y, preferred_element_type=F32) * scale).astype(o_ref.dtype)


def _dft_tables(seq):
    n2 = SEQ_DFT_MINOR
    n1 = seq // n2
    idx = jnp.arange(n2, dtype=jnp.int32)
    ang = (2.0 * math.pi / n2) * ((idx[:, None] * idx[None, :]) % n2).astype(F32)
    cm, sm = jnp.cos(ang), jnp.sin(ang)
    chan = jnp.concatenate([cm, sm], axis=1).astype(BF16)
    minor = jnp.concatenate([cm, sm], axis=1).astype(BF16)
    k1 = jnp.arange(n1, dtype=jnp.int32)
    a1 = (2.0 * math.pi / n1) * ((k1[:, None] * k1[None, :]) % n1).astype(F32)
    a2 = (2.0 * math.pi / seq) * (jnp.arange(n2, dtype=jnp.int32)[:, None] * k1[None, :]).astype(F32)
    c1, s1 = jnp.cos(a1)[None], jnp.sin(a1)[None]
    c2, s2 = jnp.cos(a2)[:, :, None], jnp.sin(a2)[:, :, None]
    tc = c1 * c2 - s1 * s2
    ts = s1 * c2 + c1 * s2
    top = jnp.concatenate([tc, -ts], axis=2)
    bot = jnp.concatenate([-ts, -tc], axis=2)
    stage1 = jnp.concatenate([top, bot], axis=1).astype(BF16)
    return chan, stage1, minor


def _fourier(f, chan, stage1, minor, *, tb):
    s, w = f.shape
    n2 = SEQ_DFT_MINOR
    n1 = s // n2
    f_view = f.reshape(n1, n2 * w)
    est1 = 2 * n1 * tb * w * 2 + 2 * tb * 4 * n1 * n1 * 2 + 2 * 2 * tb * n1 * w * 2 + 2 * n1 * w * 2 + 4 * n1 * w * 4
    y = pl.pallas_call(
        functools.partial(_fourier1_body, tb=tb),
        grid=(n2 // tb,),
        in_specs=[
            pl.BlockSpec((n1, tb * w), lambda j: (0, j)),
            pl.BlockSpec((FOURIER_GROUP_DIM, 2 * FOURIER_GROUP_DIM), lambda j: (0, 0)),
            pl.BlockSpec((tb, 2 * n1, 2 * n1), lambda j: (j, 0, 0)),
        ],
        out_specs=pl.BlockSpec((2, tb, n1, w), lambda j: (0, j, 0, 0)),
        out_shape=jax.ShapeDtypeStruct((2, n2, n1, w), BF16),
        scratch_shapes=[pltpu.VMEM((2 * n1, w), BF16)],
        compiler_params=_params(est1, 1),
        name="fourier_stage1",
    )(f_view, chan, stage1)

    tk1 = tb
    y_view = y.reshape(2, n2, n1 * w)
    scale = 1.0 / math.sqrt(float(s) * FOURIER_GROUP_DIM)
    est2 = 2 * 2 * n2 * tk1 * w * 2 + 2 * n2 * tk1 * w * 2 + 2 * n2 * tk1 * w * 4
    out = pl.pallas_call(
        functools.partial(_fourier2_body, scale=scale),
        grid=(n1 // tk1,),
        in_specs=[
            pl.BlockSpec((2, n2, tk1 * w), lambda j: (0, 0, j)),
            pl.BlockSpec((n2, 2 * n2), lambda j: (0, 0)),
        ],
        out_specs=pl.BlockSpec((n2, tk1 * w), lambda j: (0, j)),
        out_shape=jax.ShapeDtypeStruct((n2, n1 * w), BF16),
        compiler_params=_params(est2, 1),
        name="fourier_stage2",
    )(y_view, minor)
    return out.reshape(s, w)


def _merge_body(a_ref, f_ref, ga_ref, gf_ref, h_ref, wa_ref, wf_ref, wo_ref, post_g_ref, o_ref):
    ya = jnp.dot(a_ref[...], wa_ref[...], preferred_element_type=F32)
    yf = jnp.dot(f_ref[...], wf_ref[...], preferred_element_type=F32)
    merged = ga_ref[...].astype(F32) * ya + gf_ref[...].astype(F32) * yf
    z = jnp.dot(merged.astype(BF16), wo_ref[...], preferred_element_type=F32)
    o_ref[...] = h_ref[...] + _rms(z, post_g_ref[...])


def _merge(attn, mixed, gates, h, w_attn_o, w_fourier, w_out, post_g, *, tm):
    s, d = h.shape
    resident = pl.Buffered(1)
    est = ((ATTN_WIDTH + FOURIER_WIDTH + d) * d * 2
           + 2 * tm * (ATTN_WIDTH + FOURIER_WIDTH + 2 * d) * 2 + 2 * 2 * tm * d * 4 + 4 * tm * d * 4)
    return pl.pallas_call(
        _merge_body,
        grid=(s // tm,),
        in_specs=[
            pl.BlockSpec((tm, ATTN_WIDTH), lambda i: (i, 0)),
            pl.BlockSpec((tm, FOURIER_WIDTH), lambda i: (i, 0)),
            pl.BlockSpec((tm, d), lambda i: (i, 0)),
            pl.BlockSpec((tm, d), lambda i: (i, 1)),
            pl.BlockSpec((tm, d), lambda i: (i, 0)),
            pl.BlockSpec((ATTN_WIDTH, d), lambda i: (0, 0), pipeline_mode=resident),
            pl.BlockSpec((FOURIER_WIDTH, d), lambda i: (0, 0), pipeline_mode=resident),
            pl.BlockSpec((d, d), lambda i: (0, 0), pipeline_mode=resident),
            pl.BlockSpec((1, d), lambda i: (0, 0)),
        ],
        out_specs=pl.BlockSpec((tm, d), lambda i: (i, 0)),
        out_shape=jax.ShapeDtypeStruct((s, d), F32),
        compiler_params=_params(est, 1),
        name="gated_merge",
    )(attn, mixed, gates, gates, h, w_attn_o, w_fourier, w_out, post_g)


def _tiles(seq):
    return dict(
        ffn_tm=min(512, seq), ffn_tf=512,
        proj_tm=min(1024, seq), proj_tn=1024,
        attn_tq=min(512, seq), attn_tk=min(8192, seq),
        four_tb=16,
        merge_tm=min(256, seq),
    )


def kernel(x, ffn1_pre_g, ffn1_w_gate, ffn1_w_up, ffn1_w_down, ffn1_post_g, mix_pre_g, w_in, b_gate, q_norm_g, k_norm_g, w_attn_o, w_fourier, w_out, mix_post_g, ffn2_pre_g, ffn2_w_gate, ffn2_w_up, ffn2_w_down, ffn2_post_g):
    batch, seq, d = x.shape
    assert batch == 1 and seq % SEQ_DFT_MINOR == 0
    depth = w_in.shape[0]
    t = _tiles(seq)
    cos_tab, sin_tab = _rope_tables(seq)
    chan, stage1, minor = _dft_tables(seq)
    row = lambda v: v.reshape(1, -1).astype(F32)

    h = x.reshape(seq, d)
    for l in range(depth):
        h = _ffn(h, row(ffn1_pre_g[l]), ffn1_w_gate[l].astype(BF16), ffn1_w_up[l].astype(BF16),
                 ffn1_w_down[l].astype(BF16), row(ffn1_post_g[l]), tm=t["ffn_tm"], tf=t["ffn_tf"])

        n_in = w_in.shape[2]
        bias_full = jnp.concatenate([jnp.zeros((1, n_in - 2 * d), F32), row(b_gate[l])], axis=1)
        n_qk = ATTN_WIDTH + KV_WIDTH
        w_proj = w_in[l].astype(BF16)
        w_proj = w_proj.at[:, :n_qk].set(_rope_head_perm(w_proj[:, :n_qk], N_Q_HEADS + N_KV_HEADS))
        q_scale = (HEAD_DIM ** -0.5) * LOG2_E
        q_g2 = jnp.tile(_rope_head_perm(q_norm_g[l].astype(F32), 1) * q_scale, 2).reshape(1, -1)
        k_g2 = jnp.tile(_rope_head_perm(k_norm_g[l].astype(F32), 1), 2).reshape(1, -1)
        q, k, vt, f, gates = _in_proj(h, row(mix_pre_g[l]), w_proj, bias_full, q_g2, k_g2, cos_tab, sin_tab,
                                      tm=t["proj_tm"], tn=t["proj_tn"])

        attn = lax.cond(
            _score_bound(q_norm_g[l], k_norm_g[l]) <= SCORE_BOUND_LIMIT,
            functools.partial(_attention, tq=t["attn_tq"], tk=t["attn_tk"], bounded=True),
            functools.partial(_attention, tq=t["attn_tq"], tk=t["attn_tk"], bounded=False),
            q, k, vt)

        mixed = _fourier(f, chan, stage1, minor, tb=t["four_tb"])

        h = _merge(attn, mixed, gates, h, w_attn_o[l].astype(BF16), w_fourier[l].astype(BF16),
                   w_out[l].astype(BF16), row(mix_post_g[l]), tm=t["merge_tm"])

        h = _ffn(h, row(ffn2_pre_g[l]), ffn2_w_gate[l].astype(BF16), ffn2_w_up[l].astype(BF16),
                 ffn2_w_down[l].astype(BF16), row(ffn2_post_g[l]), tm=t["ffn_tm"], tf=t["ffn_tf"])
    return h.reshape(batch, seq, d)
```

```python
import functools
import math

import jax
import jax.numpy as jnp
from jax import lax
from jax.experimental import pallas as pl
from jax.experimental.pallas import tpu as pltpu

HEAD_DIM = 128
N_KV_HEADS = 4
Q_PER_KV = 4
N_Q_HEADS = N_KV_HEADS * Q_PER_KV
ATTN_WIDTH = N_Q_HEADS * HEAD_DIM
KV_WIDTH = N_KV_HEADS * HEAD_DIM
FOURIER_GROUPS = 8
FOURIER_GROUP_DIM = 128
FOURIER_WIDTH = FOURIER_GROUPS * FOURIER_GROUP_DIM
ROPE_HALF = HEAD_DIM // 4
ROPE_THETA = 10000.0
GRID_W = 64
EPS = 1e-6
MACARON_WEIGHT = 0.5
LOG2_E = 1.4426950408889634

V7X_VMEM_BYTES = 64 * 1024 * 1024
VMEM_COMPILER_RESERVE = 6 * 1024 * 1024
VMEM_TEMPORARIES = 4 * 1024 * 1024
SEQ_DFT_MINOR = 128

BF16 = jnp.bfloat16
F32 = jnp.float32


def _vmem_limit(estimate_bytes):
    return int(min(estimate_bytes * 5 // 4 + VMEM_TEMPORARIES, V7X_VMEM_BYTES - VMEM_COMPILER_RESERVE))


def _params(estimate_bytes, n_axes):
    return pltpu.CompilerParams(
        dimension_semantics=("arbitrary",) * n_axes,
        vmem_limit_bytes=_vmem_limit(estimate_bytes),
    )


def _rms(x, g):
    ms = jnp.mean(x * x, axis=-1, keepdims=True)
    return x * lax.rsqrt(ms + EPS) * g


def _ffn_body(x_ref, pre_g_ref, wg_ref, wu_ref, wd_ref, post_g_ref, o_ref, xn_ref, inv_ref):
    j = pl.program_id(1)

    @pl.when(j == 0)
    def _():
        xn_ref[...] = _rms(x_ref[...], pre_g_ref[...]).astype(xn_ref.dtype)
        o_ref[...] = jnp.zeros(o_ref.shape, o_ref.dtype)

    xn = xn_ref[...]
    half = wg_ref.shape[1] // 2
    hidden = []
    for lo in (0, half):
        gate = jnp.dot(xn, wg_ref[:, lo:lo + half], preferred_element_type=F32)
        up = jnp.dot(xn, wu_ref[:, lo:lo + half], preferred_element_type=F32)
        hidden.append((gate * jax.nn.sigmoid(gate) * up).astype(BF16))
    o_ref[...] += jnp.dot(jnp.concatenate(hidden, axis=1), wd_ref[...], preferred_element_type=F32)

    last = pl.num_programs(1) - 1

    @pl.when(j == last)
    def _():
        y = o_ref[...]
        inv_ref[...] = lax.rsqrt(jnp.mean(y * y, axis=-1, keepdims=True) + EPS)

    @pl.when(j == last)
    def _():
        o_ref[...] = x_ref[...] + o_ref[...] * inv_ref[...] * post_g_ref[...]


def _ffn(x, pre_g, w_gate, w_up, w_down, post_g, *, tm, tf):
    s, d = x.shape
    post_g = MACARON_WEIGHT * post_g
    d_ff = w_gate.shape[1]
    est = 2 * tm * d * 4 * 2 + tm * d * 2 + 2 * 3 * d * tf * 2 + 3 * tm * tf * 4 + tm * d * 4
    return pl.pallas_call(
        _ffn_body,
        grid=(s // tm, d_ff // tf),
        in_specs=[
            pl.BlockSpec((tm, d), lambda i, j: (i, 0)),
            pl.BlockSpec((1, d), lambda i, j: (0, 0)),
            pl.BlockSpec((d, tf), lambda i, j: (0, j)),
            pl.BlockSpec((d, tf), lambda i, j: (0, j)),
            pl.BlockSpec((tf, d), lambda i, j: (j, 0)),
            pl.BlockSpec((1, d), lambda i, j: (0, 0)),
        ],
        out_specs=pl.BlockSpec((tm, d), lambda i, j: (i, 0)),
        out_shape=jax.ShapeDtypeStruct((s, d), F32),
        scratch_shapes=[pltpu.VMEM((tm, d), BF16), pltpu.VMEM((tm, 1), F32)],
        compiler_params=_params(est, 2),
        name="macaron_ffn",
    )(x, pre_g, w_gate, w_up, w_down, post_g)


PROJ_PIECE = 2 * HEAD_DIM


def _norm_rope_pair(y, head_mean, g2, cos, sin):
    ms = jnp.dot((y * y).astype(BF16), head_mean, preferred_element_type=F32)
    yn = y * lax.rsqrt(ms + EPS) * g2
    outs = []
    for h in range(2):
        yh = yn[:, h * HEAD_DIM:(h + 1) * HEAD_DIM]
        outs.append(yh * cos + pltpu.roll(yh, HEAD_DIM // 2, 1) * sin)
    return outs


def _inproj_body(x_ref, pre_g_ref, w_ref, bias_ref, qg_ref, kg_ref, cos_ref, sin_ref, mean_ref,
                 q_ref, k_ref, vt_ref, f_ref, gate_ref, xn_ref, *, tn):
    j = pl.program_id(1)
    n_q_tiles = ATTN_WIDTH // tn
    kv_tile = n_q_tiles
    f_tile = kv_tile + 1

    @pl.when(j == 0)
    def _():
        xn_ref[...] = _rms(x_ref[...], pre_g_ref[...]).astype(xn_ref.dtype)

    def piece(lo, width=PROJ_PIECE):
        return jnp.dot(xn_ref[...], w_ref[:, lo:lo + width], preferred_element_type=F32)

    def normed_heads(width, g_ref):
        for lo2 in range(0, width, 2 * PROJ_PIECE):
            ys = [piece(lo) for lo in range(lo2, lo2 + 2 * PROJ_PIECE, PROJ_PIECE)]
            for p, y in enumerate(ys):
                outs = _norm_rope_pair(y, mean_ref[...], g_ref[...], cos_ref[...], sin_ref[...])
                for h, out in enumerate(outs):
                    yield (lo2 + p * PROJ_PIECE) // HEAD_DIM + h, out

    @pl.when(j < n_q_tiles)
    def _():
        for head, out in normed_heads(tn, qg_ref):
            q_ref[:, head * HEAD_DIM:(head + 1) * HEAD_DIM] = out.astype(q_ref.dtype)

    @pl.when(j == kv_tile)
    def _():
        for head, out in normed_heads(KV_WIDTH, kg_ref):
            k_ref[head] = out.astype(k_ref.dtype)
        for lo in range(KV_WIDTH, 2 * KV_WIDTH, PROJ_PIECE):
            vt_ref[lo - KV_WIDTH:lo - KV_WIDTH + PROJ_PIECE, :] = piece(lo).astype(vt_ref.dtype).T

    @pl.when(j == f_tile)
    def _():
        for lo in range(0, tn, PROJ_PIECE):
            f_ref[:, lo:lo + PROJ_PIECE] = piece(lo).astype(f_ref.dtype)

    @pl.when(j > f_tile)
    def _():
        for lo in range(0, tn, PROJ_PIECE):
            z = piece(lo) + bias_ref[:, lo:lo + PROJ_PIECE]
            gate_ref[:, lo:lo + PROJ_PIECE] = (0.5 * jnp.tanh(0.5 * z) + 0.5).astype(gate_ref.dtype)


def _in_proj(h, pre_g, w_in, bias_full, q_g2, k_g2, cos_tab, sin_tab, *, tm, tn):
    s, d = h.shape
    n = w_in.shape[1]
    assert tn == 2 * KV_WIDTH == FOURIER_WIDTH and ATTN_WIDTH % tn == 0
    n_q_tiles = ATTN_WIDTH // tn
    gate_tile0 = n_q_tiles + 2
    n_gate_tiles = n // tn - gate_tile0
    lane_head = jnp.arange(PROJ_PIECE, dtype=jnp.int32) // HEAD_DIM
    head_mean = jnp.where(lane_head[:, None] == lane_head[None, :], 1.0 / HEAD_DIM, 0.0).astype(BF16)
    est = (2 * tm * d * 4 + tm * d * 2 + 2 * d * tn * 2 + 2 * 2 * tm * tn * 2 * 2 + 2 * 2 * tm * KV_WIDTH * 2
           + 4 * tm * PROJ_PIECE * 4 + 4 * tm * HEAD_DIM * 4)
    return pl.pallas_call(
        functools.partial(_inproj_body, tn=tn),
        grid=(s // tm, n // tn),
        in_specs=[
            pl.BlockSpec((tm, d), lambda i, j: (i, 0)),
            pl.BlockSpec((1, d), lambda i, j: (0, 0)),
            pl.BlockSpec((d, tn), lambda i, j: (0, j)),
            pl.BlockSpec((1, tn), lambda i, j: (0, j)),
            pl.BlockSpec((1, PROJ_PIECE), lambda i, j: (0, 0)),
            pl.BlockSpec((1, PROJ_PIECE), lambda i, j: (0, 0)),
            pl.BlockSpec((tm, HEAD_DIM), lambda i, j: (i, 0)),
            pl.BlockSpec((tm, HEAD_DIM), lambda i, j: (i, 0)),
            pl.BlockSpec((PROJ_PIECE, PROJ_PIECE), lambda i, j: (0, 0)),
        ],
        out_specs=[
            pl.BlockSpec((tm, tn), lambda i, j: (i, jnp.minimum(j, n_q_tiles - 1))),
            pl.BlockSpec((N_KV_HEADS, tm, HEAD_DIM), lambda i, j: (0, i, 0)),
            pl.BlockSpec((KV_WIDTH, tm), lambda i, j: (0, i)),
            pl.BlockSpec((tm, FOURIER_WIDTH), lambda i, j: (i, 0)),
            pl.BlockSpec((tm, tn), lambda i, j: (i, jnp.clip(j - gate_tile0, 0, n_gate_tiles - 1))),
        ],
        out_shape=[
            jax.ShapeDtypeStruct((s, ATTN_WIDTH), BF16),
            jax.ShapeDtypeStruct((N_KV_HEADS, s, HEAD_DIM), BF16),
            jax.ShapeDtypeStruct((KV_WIDTH, s), BF16),
            jax.ShapeDtypeStruct((s, FOURIER_WIDTH), BF16),
            jax.ShapeDtypeStruct((s, n_gate_tiles * tn), BF16),
        ],
        scratch_shapes=[pltpu.VMEM((tm, d), BF16)],
        compiler_params=_params(est, 2),
        name="in_proj",
    )(h, pre_g, w_in, bias_full, q_g2, k_g2, cos_tab, sin_tab, head_mean)


def _rope_head_perm(v, n_heads):
    lead = v.shape[:-1]
    v = v.reshape(lead + (n_heads, 2, 2, ROPE_HALF))
    return jnp.swapaxes(v, -3, -2).reshape(lead + (n_heads * HEAD_DIM,))


def _rope_tables(seq):
    n_rows = seq // GRID_W
    n_freq = ROPE_HALF
    inv_freq = ROPE_THETA ** (-jnp.arange(n_freq, dtype=F32) / n_freq)
    ang_r = jnp.arange(n_rows, dtype=F32)[:, None] * inv_freq
    ang_c = jnp.arange(GRID_W, dtype=F32)[:, None] * inv_freq
    zr, zc = jnp.zeros_like(ang_r), jnp.zeros_like(ang_c)
    def table(row_part, col_part, sign):
        r = jnp.concatenate([sign * row_part, zr, row_part, zr], axis=-1)
        c = jnp.concatenate([zc, sign * col_part, zc, col_part], axis=-1)
        return (r[:, None, :] + c[None, :, :]).reshape(seq, HEAD_DIM)
    cos_tab = table(jnp.cos(ang_r), jnp.cos(ang_c), 1.0)
    sin_tab = table(jnp.sin(ang_r), jnp.sin(ang_c), -1.0)
    return cos_tab, sin_tab


SCORE_BOUND_LIMIT = 100.0


def _kv_chunks(k_ref, vt_ref, tk):
    for c in range(k_ref.shape[0] // tk):
        yield k_ref[c * tk:(c + 1) * tk, :], vt_ref[:, c * tk:(c + 1) * tk]


def _scores_t(k_c, q_ref, h):
    q_h = q_ref[:, h * HEAD_DIM:(h + 1) * HEAD_DIM]
    return lax.dot_general(k_c, q_h, (((1,), (1,)), ((), ())), preferred_element_type=F32)


def _attn_finish(o_ref, l_ref, acc_ref):
    for h in range(Q_PER_KV):
        o_t = acc_ref[h] / l_ref[h]
        o_ref[:, h * HEAD_DIM:(h + 1) * HEAD_DIM] = o_t.T.astype(o_ref.dtype)


def _attn_body_bounded(q_ref, k_ref, vt_ref, o_ref, l_ref, acc_ref, *, tk):
    l_ref[...] = jnp.zeros(l_ref.shape, F32)
    acc_ref[...] = jnp.zeros(acc_ref.shape, F32)
    for k_c, vt_c in _kv_chunks(k_ref, vt_ref, tk):
        s_next = _scores_t(k_c, q_ref, 0)
        for h in range(Q_PER_KV):
            s_t = s_next
            if h + 1 < Q_PER_KV:
                s_next = _scores_t(k_c, q_ref, h + 1)
            p_t = jnp.exp2(s_t)
            l_ref[h] += jnp.sum(p_t, axis=0, keepdims=True)
            acc_ref[h] += jnp.dot(vt_c, p_t.astype(BF16), preferred_element_type=F32)
    _attn_finish(o_ref, l_ref, acc_ref)


def _attn_body_online(q_ref, k_ref, vt_ref, o_ref, m_ref, l_ref, acc_ref, *, tk):
    m_ref[...] = jnp.full(m_ref.shape, -jnp.inf, F32)
    l_ref[...] = jnp.zeros(l_ref.shape, F32)
    acc_ref[...] = jnp.zeros(acc_ref.shape, F32)
    for k_c, vt_c in _kv_chunks(k_ref, vt_ref, tk):
        for h in range(Q_PER_KV):
            s_t = _scores_t(k_c, q_ref, h)
            m_old = m_ref[h]
            m_new = jnp.maximum(m_old, jnp.max(s_t, axis=0, keepdims=True))
            alpha = jnp.exp2(m_old - m_new)
            p_t = jnp.exp2(s_t - m_new)
            l_ref[h] = alpha * l_ref[h] + jnp.sum(p_t, axis=0, keepdims=True)
            acc_ref[h] = alpha * acc_ref[h] + jnp.dot(vt_c, p_t.astype(BF16), preferred_element_type=F32)
            m_ref[h] = m_new
    _attn_finish(o_ref, l_ref, acc_ref)


def _attention(q, k, vt, *, tq, tk, bounded):
    s = q.shape[0]
    group_w = Q_PER_KV * HEAD_DIM
    est = (2 * tq * group_w * 2 * 2 + 2 * s * 2 * HEAD_DIM * 2 + Q_PER_KV * (HEAD_DIM + 16) * tq * 4
           + 4 * tk * tq * 4)
    stat = pltpu.VMEM((Q_PER_KV, 1, tq), F32)
    acc = pltpu.VMEM((Q_PER_KV, HEAD_DIM, tq), F32)
    if bounded:
        body, name, scratch = _attn_body_bounded, "gqa_attention_bounded", [stat, acc]
    else:
        body, name, scratch = _attn_body_online, "gqa_attention_online", [stat, stat, acc]
    return pl.pallas_call(
        functools.partial(body, tk=tk),
        grid=(N_KV_HEADS, s // tq),
        in_specs=[
            pl.BlockSpec((tq, group_w), lambda g, i: (i, g)),
            pl.BlockSpec((None, s, HEAD_DIM), lambda g, i: (g, 0, 0)),
            pl.BlockSpec((HEAD_DIM, s), lambda g, i: (g, 0)),
        ],
        out_specs=pl.BlockSpec((tq, group_w), lambda g, i: (i, g)),
        out_shape=jax.ShapeDtypeStruct((s, ATTN_WIDTH), BF16),
        scratch_shapes=scratch,
        compiler_params=_params(est, 2),
        name=name,
    )(q, k, vt)


def _score_bound(q_g, k_g):
    q_scale = (HEAD_DIM ** -0.5) * LOG2_E
    return HEAD_DIM * q_scale * jnp.max(jnp.abs(q_g)) * jnp.max(jnp.abs(k_g))


def _fourier1_body(f_ref, chan_ref, t_ref, y_ref, ab_ref, *, tb):
    n1 = f_ref.shape[0]
    for b in range(tb):
        for g in range(FOURIER_GROUPS):
            lo = b * FOURIER_WIDTH + g * FOURIER_GROUP_DIM
            ab = jnp.dot(f_ref[:, lo:lo + FOURIER_GROUP_DIM], chan_ref[...], preferred_element_type=F32)
            cols = slice(g * FOURIER_GROUP_DIM, (g + 1) * FOURIER_GROUP_DIM)
            ab_ref[:n1, cols] = ab[:, :FOURIER_GROUP_DIM].astype(ab_ref.dtype)
            ab_ref[n1:, cols] = ab[:, FOURIER_GROUP_DIM:].astype(ab_ref.dtype)
        y = jnp.dot(t_ref[b], ab_ref[...], preferred_element_type=F32)
        y_ref[0, b] = y[:n1].astype(y_ref.dtype)
        y_ref[1, b] = y[n1:].astype(y_ref.dtype)


def _fourier2_body(y_ref, f2_ref, o_ref, *, scale):
    two, n2, w = y_ref.shape
    y = y_ref[...].reshape(two * n2, w)
    o_ref[...] = (jnp.dot(f2_ref[...], y, preferred_element_type=F32) * scale).astype(o_ref.dtype)


def _dft_tables(seq):
    n2 = SEQ_DFT_MINOR
    n1 = seq // n2
    idx = jnp.arange(n2, dtype=jnp.int32)
    ang = (2.0 * math.pi / n2) * ((idx[:, None] * idx[None, :]) % n2).astype(F32)
    cm, sm = jnp.cos(ang), jnp.sin(ang)
    chan = jnp.concatenate([cm, sm], axis=1).astype(BF16)
    minor = jnp.concatenate([cm, sm], axis=1).astype(BF16)
    k1 = jnp.arange(n1, dtype=jnp.int32)
    a1 = (2.0 * math.pi / n1) * ((k1[:, None] * k1[None, :]) % n1).astype(F32)
    a2 = (2.0 * math.pi / seq) * (jnp.arange(n2, dtype=jnp.int32)[:, None] * k1[None, :]).astype(F32)
    c1, s1 = jnp.cos(a1)[None], jnp.sin(a1)[None]
    c2, s2 = jnp.cos(a2)[:, :, None], jnp.sin(a2)[:, :, None]
    tc = c1 * c2 - s1 * s2
    ts = s1 * c2 + c1 * s2
    top = jnp.concatenate([tc, -ts], axis=2)
    bot = jnp.concatenate([-ts, -tc], axis=2)
    stage1 = jnp.concatenate([top, bot], axis=1).astype(BF16)
    return chan, stage1, minor


def _fourier(f, chan, stage1, minor, *, tb):
    s, w = f.shape
    n2 = SEQ_DFT_MINOR
    n1 = s // n2
    f_view = f.reshape(n1, n2 * w)
    est1 = 2 * n1 * tb * w * 2 + 2 * tb * 4 * n1 * n1 * 2 + 2 * 2 * tb * n1 * w * 2 + 2 * n1 * w * 2 + 4 * n1 * w * 4
    y = pl.pallas_call(
        functools.partial(_fourier1_body, tb=tb),
        grid=(n2 // tb,),
        in_specs=[
            pl.BlockSpec((n1, tb * w), lambda j: (0, j)),
            pl.BlockSpec((FOURIER_GROUP_DIM, 2 * FOURIER_GROUP_DIM), lambda j: (0, 0)),
            pl.BlockSpec((tb, 2 * n1, 2 * n1), lambda j: (j, 0, 0)),
        ],
        out_specs=pl.BlockSpec((2, tb, n1, w), lambda j: (0, j, 0, 0)),
        out_shape=jax.ShapeDtypeStruct((2, n2, n1, w), BF16),
        scratch_shapes=[pltpu.VMEM((2 * n1, w), BF16)],
        compiler_params=_params(est1, 1),
        name="fourier_stage1",
    )(f_view, chan, stage1)

    tk1 = tb
    y_view = y.reshape(2, n2, n1 * w)
    scale = 1.0 / math.sqrt(float(s) * FOURIER_GROUP_DIM)
    est2 = 2 * 2 * n2 * tk1 * w * 2 + 2 * n2 * tk1 * w * 2 + 2 * n2 * tk1 * w * 4
    out = pl.pallas_call(
        functools.partial(_fourier2_body, scale=scale),
        grid=(n1 // tk1,),
        in_specs=[
            pl.BlockSpec((2, n2, tk1 * w), lambda j: (0, 0, j)),
            pl.BlockSpec((n2, 2 * n2), lambda j: (0, 0)),
        ],
        out_specs=pl.BlockSpec((n2, tk1 * w), lambda j: (0, j)),
        out_shape=jax.ShapeDtypeStruct((n2, n1 * w), BF16),
        compiler_params=_params(est2, 1),
        name="fourier_stage2",
    )(y_view, minor)
    return out.reshape(s, w)


def _merge_body(a_ref, f_ref, ga_ref, gf_ref, h_ref, wa_ref, wf_ref, wo_ref, post_g_ref, o_ref):
    ya = jnp.dot(a_ref[...], wa_ref[...], preferred_element_type=F32)
    yf = jnp.dot(f_ref[...], wf_ref[...], preferred_element_type=F32)
    merged = ga_ref[...].astype(F32) * ya + gf_ref[...].astype(F32) * yf
    z = jnp.dot(merged.astype(BF16), wo_ref[...], preferred_element_type=F32)
    o_ref[...] = h_ref[...] + _rms(z, post_g_ref[...])


def _merge(attn, mixed, gates, h, w_attn_o, w_fourier, w_out, post_g, *, tm):
    s, d = h.shape
    resident = pl.Buffered(1)
    est = ((ATTN_WIDTH + FOURIER_WIDTH + d) * d * 2
           + 2 * tm * (ATTN_WIDTH + FOURIER_WIDTH + 2 * d) * 2 + 2 * 2 * tm * d * 4 + 4 * tm * d * 4)
    return pl.pallas_call(
        _merge_body,
        grid=(s // tm,),
        in_specs=[
            pl.BlockSpec((tm, ATTN_WIDTH), lambda i: (i, 0)),
            pl.BlockSpec((tm, FOURIER_WIDTH), lambda i: (i, 0)),
            pl.BlockSpec((tm, d), lambda i: (i, 0)),
            pl.BlockSpec((tm, d), lambda i: (i, 1)),
            pl.BlockSpec((tm, d), lambda i: (i, 0)),
            pl.BlockSpec((ATTN_WIDTH, d), lambda i: (0, 0), pipeline_mode=resident),
            pl.BlockSpec((FOURIER_WIDTH, d), lambda i: (0, 0), pipeline_mode=resident),
            pl.BlockSpec((d, d), lambda i: (0, 0), pipeline_mode=resident),
            pl.BlockSpec((1, d), lambda i: (0, 0)),
        ],
        out_specs=pl.BlockSpec((tm, d), lambda i: (i, 0)),
        out_shape=jax.ShapeDtypeStruct((s, d), F32),
        compiler_params=_params(est, 1),
        name="gated_merge",
    )(attn, mixed, gates, gates, h, w_attn_o, w_fourier, w_out, post_g)


def _tiles(seq):
    return dict(
        ffn_tm=min(512, seq), ffn_tf=512,
        proj_tm=min(1024, seq), proj_tn=1024,
        attn_tq=min(512, seq), attn_tk=min(8192, seq),
        four_tb=16,
        merge_tm=min(256, seq),
    )


def kernel(x, ffn1_pre_g, ffn1_w_gate, ffn1_w_up, ffn1_w_down, ffn1_post_g, mix_pre_g, w_in, b_gate, q_norm_g, k_norm_g, w_attn_o, w_fourier, w_out, mix_post_g, ffn2_pre_g, ffn2_w_gate, ffn2_w_up, ffn2_w_down, ffn2_post_g):
    batch, seq, d = x.shape
    assert batch == 1 and seq % SEQ_DFT_MINOR == 0
    depth = w_in.shape[0]
    t = _tiles(seq)
    cos_tab, sin_tab = _rope_tables(seq)
    chan, stage1, minor = _dft_tables(seq)
    row = lambda v: v.reshape(1, -1).astype(F32)

    h = x.reshape(seq, d)
    for l in range(depth):
        h = _ffn(h, row(ffn1_pre_g[l]), ffn1_w_gate[l].astype(BF16), ffn1_w_up[l].astype(BF16),
                 ffn1_w_down[l].astype(BF16), row(ffn1_post_g[l]), tm=t["ffn_tm"], tf=t["ffn_tf"])

        n_in = w_in.shape[2]
        bias_full = jnp.concatenate([jnp.zeros((1, n_in - 2 * d), F32), row(b_gate[l])], axis=1)
        n_qk = ATTN_WIDTH + KV_WIDTH
        w_proj = w_in[l].astype(BF16)
        w_proj = w_proj.at[:, :n_qk].set(_rope_head_perm(w_proj[:, :n_qk], N_Q_HEADS + N_KV_HEADS))
        q_scale = (HEAD_DIM ** -0.5) * LOG2_E
        q_g2 = jnp.tile(_rope_head_perm(q_norm_g[l].astype(F32), 1) * q_scale, 2).reshape(1, -1)
        k_g2 = jnp.tile(_rope_head_perm(k_norm_g[l].astype(F32), 1), 2).reshape(1, -1)
        q, k, vt, f, gates = _in_proj(h, row(mix_pre_g[l]), w_proj, bias_full, q_g2, k_g2, cos_tab, sin_tab,
                                      tm=t["proj_tm"], tn=t["proj_tn"])

        attn = lax.cond(
            _score_bound(q_norm_g[l], k_norm_g[l]) <= SCORE_BOUND_LIMIT,
            functools.partial(_attention, tq=t["attn_tq"], tk=t["attn_tk"], bounded=True),
            functools.partial(_attention, tq=t["attn_tq"], tk=t["attn_tk"], bounded=False),
            q, k, vt)

        mixed = _fourier(f, chan, stage1, minor, tb=t["four_tb"])

        h = _merge(attn, mixed, gates, h, w_attn_o[l].astype(BF16), w_fourier[l].astype(BF16),
                   w_out[l].astype(BF16), row(mix_post_g[l]), tm=t["merge_tm"])

        h = _ffn(h, row(ffn2_pre_g[l]), ffn2_w_gate[l].astype(BF16), ffn2_w_up[l].astype(BF16),
                 ffn2_w_down[l].astype(BF16), row(ffn2_post_g[l]), tm=t["ffn_tm"], tf=t["ffn_tf"])
    return h.reshape(batch, seq, d)
```
